```python
import math
import jax, jax.numpy as jnp
from jax import lax
import numpy as np

D_MODEL = 2048
BATCH = 4
SEQ = 4096
DEPTH = 4

MIX_WIDTH = D_MODEL
HG_WIDTH = MIX_WIDTH // 2
DA_WIDTH = MIX_WIDTH - HG_WIDTH
HG_HEADS = 8
HG_DK = HG_WIDTH // HG_HEADS
HG_DV = HG_WIDTH // HG_HEADS
HG_CHUNK = 64
DA_HEADS = 8
DA_DV = DA_WIDTH // DA_HEADS
DA_DH = DA_DV // 2
Q_BLOCK = 128
D_FF = -(-8 * D_MODEL // (3 * 256)) * 256
EPS = 1e-6
LB_FLOOR = 1e-30
LB_CEIL = 1.0 - 1e-6
SPLIT_WIDTHS = [HG_WIDTH] * 5 + [DA_WIDTH] * 3
W_IN_COLS = sum(SPLIT_WIDTHS)

kernel_name = 'hybrid_hgrn2_diffattn_encoder'


def rmsnorm(x, w):
    xf = x.astype(jnp.float32)
    var = jnp.mean(xf * xf, axis=-1, keepdims=True)
    return (xf * lax.rsqrt(var + EPS)).astype(x.dtype) * w


def alibi_slopes(n):
    return jnp.asarray([2.0 ** (-8.0 * (i + 1) / n) for i in range(n)], dtype=jnp.float32)


def log_forget(z, lb):
    lb = jnp.clip(lb, 0.0, LB_CEIL)
    return jnp.logaddexp(jnp.log(jnp.maximum(lb, LB_FLOOR)),
                         jnp.log1p(-lb) + jax.nn.log_sigmoid(z))


def gla_chunk_scan(q, k, v, log_f):
    b, h, s, dk = q.shape
    dv = v.shape[-1]
    nc = s // HG_CHUNK

    def to_chunks(t):
        return jnp.moveaxis(t.reshape(b, h, nc, HG_CHUNK, t.shape[-1]), 2, 0)

    tri = jnp.tril(jnp.ones((HG_CHUNK, HG_CHUNK), dtype=bool))[:, :, None]

    def step(state, inp):
        qc, kc, vc, gc = inp
        a = jnp.cumsum(gc, axis=2)
        o_inter = jnp.einsum('bhtd,bhde->bhte', qc * jnp.exp(a), state)
        diff = a[:, :, :, None, :] - a[:, :, None, :, :]
        decay = jnp.where(tri, jnp.exp(jnp.minimum(diff, 0.0)), 0.0)
        scores = jnp.einsum('bhtd,bhsd,bhtsd->bhts', qc, kc, decay)
        o_intra = jnp.einsum('bhts,bhse->bhte', scores, vc)
        a_last = a[:, :, -1:, :]
        state = (jnp.exp(a_last)[:, :, 0, :, None] * state
                 + jnp.einsum('bhsd,bhse->bhde', kc * jnp.exp(a_last - a), vc))
        return state, o_inter + o_intra

    init = jnp.zeros((b, h, dk, dv), dtype=q.dtype)
    _, o = lax.scan(step, init, (to_chunks(q), to_chunks(k), to_chunks(v), to_chunks(log_f)))
    return jnp.moveaxis(o, 0, 2).reshape(b, h, s, dv)


def hgrn2_mixer(q, zf_fwd, zf_bwd, v, g, lb_fwd, lb_bwd, onorm_w):
    bsz, slen = q.shape[0], q.shape[1]

    def heads(t):
        return jnp.transpose(t.reshape(bsz, slen, HG_HEADS, -1).astype(jnp.float32), (0, 2, 1, 3))

    qh, vh = heads(q), heads(v)

    def one_direction(zf, lb, flip):
        lf = log_forget(heads(zf), lb.reshape(HG_HEADS, 1, HG_DK))
        kh = -jnp.expm1(lf)
        args = (qh, kh, vh, lf)
        if flip:
            args = tuple(jnp.flip(t, axis=2) for t in args)
        o = gla_chunk_scan(*args)
        return jnp.flip(o, axis=2) if flip else o

    o = one_direction(zf_fwd, lb_fwd, False) + one_direction(zf_bwd, lb_bwd, True)
    o = jnp.transpose(o, (0, 2, 1, 3))
    o = rmsnorm(o, onorm_w) * jax.nn.silu(g.reshape(o.shape).astype(jnp.float32))
    return o.reshape(bsz, slen, HG_WIDTH)


def diff_attention(q, k, v, qn_w, kn_w, lam, lam_init, subln_w):
    bsz, slen = q.shape[0], q.shape[1]
    q = rmsnorm(q.reshape(bsz, slen, DA_HEADS, 2, DA_DH), qn_w) * (DA_DH ** -0.5)
    k = rmsnorm(k.reshape(bsz, slen, DA_HEADS, 2, DA_DH), kn_w)
    v = v.reshape(bsz, slen, DA_HEADS, DA_DV)
    slopes = alibi_slopes(DA_HEADS)
    nb = slen // Q_BLOCK
    q_blocks = jnp.moveaxis(q.reshape(bsz, nb, Q_BLOCK, DA_HEADS, 2, DA_DH), 1, 0)
    starts = jnp.arange(nb, dtype=jnp.int32) * Q_BLOCK
    pos_k = jnp.arange(slen, dtype=jnp.int32)

    def block(args):
        q_blk, start = args
        s = jnp.einsum('bthmd,bshmd->bhmts', q_blk, k).astype(jnp.float32)
        pos_q = start + jnp.arange(Q_BLOCK, dtype=jnp.int32)
        dist = jnp.abs(pos_q[:, None] - pos_k[None, :]).astype(jnp.float32)
        s = s - slopes[None, :, None, None, None] * dist
        p = jax.nn.softmax(s, axis=-1)
        attn = p[:, :, 0] - lam * p[:, :, 1]
        return jnp.einsum('bhts,bshe->bthe', attn.astype(v.dtype), v)

    o = lax.map(block, (q_blocks, starts))
    o = jnp.moveaxis(o, 0, 1).reshape(bsz, slen, DA_HEADS, DA_DV)
    o = rmsnorm(o, subln_w) * (1.0 - lam_init)
    return o.reshape(bsz, slen, DA_WIDTH)


def setup_inputs(seed: int = 0) -> dict:
    key = jax.random.key(seed)
    ks = jax.random.split(key, 16)
    f32 = jnp.float32

    def nrm(k, shape, scale):
        return jax.random.normal(k, shape, dtype=f32) * scale

    return {
        'x': nrm(ks[0], (BATCH, SEQ, D_MODEL), 1.0),
        'norm_mix_w': 1.0 + nrm(ks[1], (DEPTH, D_MODEL), 0.02),
        'w_in': nrm(ks[2], (DEPTH, D_MODEL, W_IN_COLS), D_MODEL ** -0.5),
        'hg_lb_logits': 1.0 + nrm(ks[3], (2, DEPTH, HG_WIDTH), 0.1),
        'hg_onorm_w': 1.0 + nrm(ks[4], (DEPTH, HG_DV), 0.02),
        'da_qnorm_w': 1.0 + nrm(ks[5], (DEPTH, DA_DH), 0.02),
        'da_knorm_w': 1.0 + nrm(ks[6], (DEPTH, DA_DH), 0.02),
        'da_lambda': nrm(ks[7], (DEPTH, 4, DA_DH), 0.1),
        'da_subln_w': 1.0 + nrm(ks[8], (DEPTH, DA_DV), 0.02),
        'w_out': nrm(ks[9], (DEPTH, MIX_WIDTH, D_MODEL), MIX_WIDTH ** -0.5),
        'norm_ffn_w': 1.0 + nrm(ks[10], (DEPTH, D_MODEL), 0.02),
        'w_gate': nrm(ks[11], (DEPTH, D_MODEL, D_FF), D_MODEL ** -0.5),
        'w_up': nrm(ks[12], (DEPTH, D_MODEL, D_FF), D_MODEL ** -0.5),
        'w_down': nrm(ks[13], (DEPTH, D_FF, D_MODEL), D_FF ** -0.5),
    }


def reference(x, norm_mix_w, w_in, hg_lb_logits, hg_onorm_w, da_qnorm_w, da_knorm_w,
              da_lambda, da_subln_w, w_out, norm_ffn_w, w_gate, w_up, w_down):
    p_lb = jax.nn.softmax(hg_lb_logits.astype(jnp.float32), axis=1)
    lower_bounds = jnp.cumsum(p_lb, axis=1) - p_lb[:, :1]
    split_idx = list(np.cumsum(SPLIT_WIDTHS)[:-1].tolist())

    for l in range(DEPTH):
        h = rmsnorm(x, norm_mix_w[l])
        proj = jnp.einsum('bsd,dc->bsc', h, w_in[l])
        hq, hf_f, hf_b, hi, hg, dq, dk, dv = jnp.split(proj, split_idx, axis=-1)

        o_hg = hgrn2_mixer(hq, hf_f, hf_b, hi, hg,
                           lower_bounds[0, l], lower_bounds[1, l], hg_onorm_w[l])

        lam_p = da_lambda[l].astype(jnp.float32)
        lam_init = 0.8 - 0.6 * math.exp(-0.3 * l)
        lam = (jnp.exp(jnp.sum(lam_p[0] * lam_p[1])) - jnp.exp(jnp.sum(lam_p[2] * lam_p[3]))
               + lam_init)
        o_da = diff_attention(dq, dk, dv, da_qnorm_w[l], da_knorm_w[l], lam, lam_init,
                              da_subln_w[l])

        mixed = jnp.concatenate([o_hg.astype(x.dtype), o_da.astype(x.dtype)], axis=-1)
        x = x + jnp.einsum('bsc,cd->bsd', mixed, w_out[l])

        h = rmsnorm(x, norm_ffn_w[l])
        u = jax.nn.silu(jnp.einsum('bsd,df->bsf', h, w_gate[l])) * jnp.einsum('bsd,df->bsf', h, w_up[l])
        x = x + jnp.einsum('bsf,fd->bsd', u, w_down[l])
    return x
```

```python
import functools
import math

import jax
import jax.numpy as jnp
from jax import lax
from jax.experimental import pallas as pl
from jax.experimental.pallas import tpu as pltpu

F32 = jnp.float32
BF16 = jnp.bfloat16

LANES = 128
HEADS = 8
EPS = 1e-6
LB_FLOOR = 1e-30
LB_CEIL = 1.0 - 1e-6
CHUNK = 64
SUB = 16
EXP_CLAMP = 80.0
VMEM_CAP = 60 * 1024 * 1024
VMEM_SLACK = 12 * 1024 * 1024


def _vmem_limit(pipelined_bytes, scratch_bytes=0):
    return int(min(VMEM_CAP, 2 * pipelined_bytes + scratch_bytes + VMEM_SLACK))


def _nbytes(shape, dtype):
    return math.prod(shape) * jnp.dtype(dtype).itemsize


def _dot(a, b):
    return jnp.dot(a, b, preferred_element_type=F32)


def _dot_nt(a, b):
    return lax.dot_general(a, b, (((1,), (1,)), ((), ())), preferred_element_type=F32)


def _dot_tn(a, b):
    return lax.dot_general(a, b, (((0,), (0,)), ((), ())), preferred_element_type=F32)


def _sigmoid(x):
    return 1.0 / (1.0 + jnp.exp(-x))


def _proj_kernel(x_ref, nw_ref, w_ref, o_ref, h_s):
    @pl.when(pl.program_id(1) == 0)
    def _():
        x = x_ref[...]
        var = jnp.mean(x * x, axis=-1, keepdims=True)
        h_s[...] = (x * lax.rsqrt(var + EPS) * nw_ref[...]).astype(BF16)

    r = _dot(h_s[...], w_ref[...])
    for c in range(o_ref.shape[0]):
        o_ref[c] = r[:, c * LANES:(c + 1) * LANES].astype(o_ref.dtype)


def _proj(x, nw, w, out_dtype, *, tm=1024, tn=512):
    n, d = x.shape
    cols = w.shape[1]
    tm = min(tm, n)
    grid = (n // tm, cols // tn)
    blocks = (_nbytes((tm, d), F32) + _nbytes((d, tn), BF16) + _nbytes((tm, tn), out_dtype))
    return pl.pallas_call(
        _proj_kernel,
        grid=grid,
        in_specs=[
            pl.BlockSpec((tm, d), lambda i, j: (i, 0)),
            pl.BlockSpec((1, d), lambda i, j: (0, 0)),
            pl.BlockSpec((d, tn), lambda i, j: (0, j)),
        ],
        out_specs=pl.BlockSpec((tn // LANES, tm, LANES), lambda i, j: (j, i, 0)),
        out_shape=jax.ShapeDtypeStruct((cols // LANES, n, LANES), out_dtype),
        scratch_shapes=[pltpu.VMEM((tm, d), BF16)],
        compiler_params=pltpu.CompilerParams(
            dimension_semantics=("parallel", "arbitrary"),
            vmem_limit_bytes=_vmem_limit(blocks, _nbytes((tm, d), BF16) + _nbytes((tm, tn), F32))),
        name="proj",
    )(x, nw.reshape(1, d), w)


def _seg_cumsum(x, seg, rev):
    n = x.shape[0]
    pos = lax.broadcasted_iota(jnp.int32, x.shape, 0) & (seg - 1)
    s = 1
    while s < seg:
        if rev:
            x = x + jnp.where(pos < seg - s, pltpu.roll(x, n - s, 0), 0.0)
        else:
            x = x + jnp.where(pos >= s, pltpu.roll(x, s, 0), 0.0)
        s *= 2
    return x


def _rows_bcast(x, rows, reps):
    return jnp.concatenate([jnp.broadcast_to(x[r:r + 1, :], (reps, x.shape[1])) for r in rows], axis=0)


def _hgrn_kernel(q_ref, v_ref, g_ref, zf_ref, zb_ref, lbl_ref, ow_ref, o_ref,
                 oacc_s, k_s, a_s, v_s, intra_s, *, layer, depth, tblk):
    seq = q_ref.shape[0]
    nblk = seq // tblk
    nchunk = tblk // CHUNK
    nsub = CHUNK // SUB

    def lower_bound(direction):
        lg = lbl_ref[direction * depth:(direction + 1) * depth, :]
        e = jnp.exp(lg - jnp.max(lg, axis=0, keepdims=True))
        p = e / jnp.sum(e, axis=0, keepdims=True)
        lb = jnp.sum(p[:layer + 1], axis=0, keepdims=True) - p[0:1]
        lb = jnp.clip(lb, 0.0, LB_CEIL)
        return lb, jnp.maximum(lb, LB_FLOOR)

    row_c = lax.broadcasted_iota(jnp.int32, (CHUNK, LANES), 0)
    sr = lax.broadcasted_iota(jnp.int32, (CHUNK, CHUNK), 0)
    sc_ = lax.broadcasted_iota(jnp.int32, (CHUNK, CHUNK), 1)

    def run_pass(z_ref, direction, rev, finalize):
        lb, lbp = lower_bound(direction)
        causal = (sc_ >= sr) if rev else (sc_ <= sr)

        def body(i, st):
            blk = (nblk - 1 - i) if rev else i
            r0 = pl.multiple_of(blk * tblk, tblk)
            q = q_ref[pl.ds(r0, tblk), :].astype(F32)
            vb = v_ref[pl.ds(r0, tblk), :]
            z = z_ref[pl.ds(r0, tblk), :]
            sg = _sigmoid(z)
            lf = jnp.log(lbp + (1.0 - lb) * sg)
            k = (1.0 - lb) * (1.0 - sg) - (lbp - lb)
            a = _seg_cumsum(lf, CHUNK, rev)
            ex = a - lf
            edge = (lambda c: c * CHUNK) if rev else (lambda c: c * CHUNK + CHUNK - 1)
            subrow = (lambda j: j * SUB + SUB - 1) if rev else (lambda j: j * SUB)
            a_edge = _rows_bcast(a, [edge(c) for c in range(nchunk)], CHUNK)
            ref_sub = _rows_bcast(ex, [subrow(j) for j in range(tblk // SUB)], SUB)
            d = ref_sub - a
            overflow = jnp.max(d) > EXP_CLAMP
            qs = (q * jnp.exp(-d)).astype(BF16)
            qa = (q * jnp.exp(a)).astype(BF16)
            kb = (k * jnp.exp(a_edge - a)).astype(BF16)

            inter = [None] * nchunk
            intra = [None] * nchunk
            for c in (range(nchunk - 1, -1, -1) if rev else range(nchunk)):
                lo = c * CHUNK
                inter[c] = _dot_nt(qa[lo:lo + CHUNK], st.astype(BF16))
                upd = _dot_tn(vb[lo:lo + CHUNK], kb[lo:lo + CHUNK])
                st = st * jnp.exp(a[edge(c):edge(c) + 1, :]) + upd
                a_c = a[lo:lo + CHUNK]
                k_c = k[lo:lo + CHUNK]
                parts = []
                for j in range(nsub):
                    rr = lo + subrow(j)
                    keep = (row_c >= j * SUB) if rev else (row_c < (j + 1) * SUB)
                    e = jnp.where(keep, jnp.exp(jnp.minimum(ex[rr:rr + 1, :] - a_c, EXP_CLAMP)), 0.0)
                    parts.append(_dot_nt(qs[lo + j * SUB:lo + (j + 1) * SUB], (k_c * e).astype(BF16)))
                scores = jnp.where(causal, jnp.concatenate(parts, axis=0), 0.0).astype(BF16)
                intra[c] = _dot(scores, vb[lo:lo + CHUNK])
            intra_s[...] = jnp.concatenate(intra, axis=0)

            @pl.when(overflow)
            def _():
                k_s[...] = k
                a_s[...] = a
                v_s[...] = vb.astype(F32)
                for c in range(nchunk):
                    lo = c * CHUNK
                    q_c = q[lo:lo + CHUNK]
                    a_c = a[lo:lo + CHUNK]

                    def pair(s, acc):
                        w = q_c * k_s[pl.ds(lo + s, 1), :] * jnp.exp(
                            jnp.minimum(a_c - a_s[pl.ds(lo + s, 1), :], 0.0))
                        r = jnp.sum(w, axis=1, keepdims=True)
                        m = (row_c <= s) if rev else (row_c >= s)
                        return acc + jnp.where(m, r, 0.0) * v_s[pl.ds(lo + s, 1), :]

                    intra_s[lo:lo + CHUNK, :] = lax.fori_loop(
                        0, CHUNK, pair, jnp.zeros((CHUNK, LANES), F32))

            o = jnp.concatenate(inter, axis=0) + intra_s[...]
            if finalize:
                o = o + oacc_s[pl.ds(r0, tblk), :]
                var = jnp.mean(o * o, axis=-1, keepdims=True)
                gate = g_ref[pl.ds(r0, tblk), :].astype(F32)
                o = o * lax.rsqrt(var + EPS) * ow_ref[...] * (gate * _sigmoid(gate))
                o_ref[pl.ds(r0, tblk), :] = o.astype(o_ref.dtype)
            else:
                oacc_s[pl.ds(r0, tblk), :] = o
            return st

        lax.fori_loop(0, nblk, body, jnp.zeros((LANES, LANES), F32))

    run_pass(zf_ref, 0, False, False)
    run_pass(zb_ref, 1, True, True)


def _hgrn(pb, pz, lb_logits, onorm_w, *, layer, batch, seq, tblk=256):
    depth = lb_logits.shape[0] // 2
    n = batch * seq
    tblk = min(tblk, seq)

    def slab(base):
        return pl.BlockSpec((None, seq, LANES), lambda b, h: (base + h, b, 0))

    blocks = 4 * _nbytes((seq, LANES), BF16) + 2 * _nbytes((seq, LANES), F32)
    scratch = _nbytes((seq, LANES), F32) + 4 * _nbytes((tblk, LANES), F32)
    return pl.pallas_call(
        functools.partial(_hgrn_kernel, layer=layer, depth=depth, tblk=tblk),
        grid=(batch, HEADS),
        in_specs=[
            slab(0),
            slab(HEADS),
            slab(2 * HEADS),
            pl.BlockSpec((None, seq, LANES), lambda b, h: (h, b, 0)),
            pl.BlockSpec((None, seq, LANES), lambda b, h: (HEADS + h, b, 0)),
            pl.BlockSpec((2 * depth, LANES), lambda b, h: (0, h)),
            pl.BlockSpec((1, LANES), lambda b, h: (0, 0)),
        ],
        out_specs=pl.BlockSpec((None, seq, LANES), lambda b, h: (h, b, 0)),
        out_shape=jax.ShapeDtypeStruct((HEADS, n, LANES), BF16),
        scratch_shapes=[
            pltpu.VMEM((seq, LANES), F32),
            pltpu.VMEM((tblk, LANES), F32),
            pltpu.VMEM((tblk, LANES), F32),
            pltpu.VMEM((tblk, LANES), F32),
            pltpu.VMEM((tblk, LANES), F32),
        ],
        compiler_params=pltpu.CompilerParams(
            dimension_semantics=("parallel", "parallel"),
            vmem_limit_bytes=_vmem_limit(blocks, scratch)),
        name="hgrn",
    )(pb, pb, pb, pz, pz, lb_logits, onorm_w.reshape(1, LANES))


def _half_rms_scale(x, lo_mask):
    sq = x * x
    half = x.shape[1] // 2
    ss_lo = jnp.sum(jnp.where(lo_mask, sq, 0.0), axis=1, keepdims=True)
    ss_hi = jnp.sum(jnp.where(lo_mask, 0.0, sq), axis=1, keepdims=True)
    return jnp.where(lo_mask, lax.rsqrt(ss_lo / half + EPS), lax.rsqrt(ss_hi / half + EPS))


def _attn_kernel(q_ref, k_ref, v_ref, qw_ref, kw_ref, lam_ref, sw_ref, o_ref,
                 kn_s, m_s, l_s, acc_s, *, tk, lam_init):
    tq = q_ref.shape[0]
    seq = k_ref.shape[0]
    h = pl.program_id(1)
    qi = pl.program_id(2)
    lo_mask = lax.broadcasted_iota(jnp.int32, (1, LANES), 1) < LANES // 2

    @pl.when(qi == 0)
    def _():
        for c in range(seq // tk):
            kf = k_ref[c * tk:(c + 1) * tk, :].astype(F32)
            kn_s[c * tk:(c + 1) * tk, :] = (kf * _half_rms_scale(kf, lo_mask) * kw_ref[...]).astype(BF16)

    qf = q_ref[...].astype(F32)
    qn = qf * _half_rms_scale(qf, lo_mask) * qw_ref[...] * ((LANES // 2) ** -0.5)
    q_both = jnp.concatenate([jnp.where(lo_mask, qn, 0.0), jnp.where(lo_mask, 0.0, qn)],
                             axis=0).astype(BF16)

    m_s[...] = jnp.full(m_s.shape, -jnp.inf, F32)
    l_s[...] = jnp.zeros(l_s.shape, F32)
    acc_s[...] = jnp.zeros(acc_s.shape, F32)

    slope = lax.bitcast_convert_type(jnp.full((1, tk), (126 - h) << 23, jnp.int32), F32)
    rel = (lax.broadcasted_iota(jnp.int32, (tq, tk), 0)
           - lax.broadcasted_iota(jnp.int32, (tq, tk), 1)).astype(F32)

    def body(kj, carry):
        k0 = pl.multiple_of(kj * tk, tk)
        s = _dot_nt(q_both, kn_s[pl.ds(k0, tk), :])
        bias = jnp.abs(rel + (qi * tq - kj * tk).astype(F32)) * (-slope)
        s = s + jnp.concatenate([bias, bias], axis=0)
        m_prev = m_s[...]
        m_new = jnp.maximum(m_prev, jnp.max(s, axis=1, keepdims=True))
        alpha = jnp.exp(m_prev - m_new)
        p = jnp.exp(s - m_new)
        l_s[...] = alpha * l_s[...] + jnp.sum(p, axis=1, keepdims=True)
        acc_s[...] = alpha * acc_s[...] + _dot(p.astype(BF16), v_ref[pl.ds(k0, tk), :])
        m_s[...] = m_new
        return carry

    lax.fori_loop(0, seq // tk, body, 0)

    lam_p = lam_ref[...]
    lam = (jnp.exp(jnp.sum(lam_p[0:1] * lam_p[1:2], axis=1, keepdims=True))
           - jnp.exp(jnp.sum(lam_p[2:3] * lam_p[3:4], axis=1, keepdims=True)) + lam_init)
    acc = acc_s[...]
    l = l_s[...]
    o = acc[:tq] / l[:tq] - lam * (acc[tq:] / l[tq:])
    var = jnp.mean(o * o, axis=-1, keepdims=True)
    o = o * lax.rsqrt(var + EPS) * sw_ref[...] * (1.0 - lam_init)
    o_ref[...] = o.astype(o_ref.dtype)


def _attn(pb, qn_w, kn_w, lam_p, subln_w, *, layer, batch, seq, tq=256, tk=512):
    n = batch * seq
    tq = min(tq, seq)
    tk = min(tk, seq)
    nq = seq // tq
    lam_init = 0.8 - 0.6 * math.exp(-0.3 * layer)
    blocks = 2 * _nbytes((tq, LANES), BF16) + 2 * _nbytes((seq, LANES), BF16)
    scratch = (_nbytes((seq, LANES), BF16) + 3 * _nbytes((2 * tq, LANES), F32)
               + 4 * _nbytes((2 * tq, tk), F32))
    return pl.pallas_call(
        functools.partial(_attn_kernel, tk=tk, lam_init=lam_init),
        grid=(batch, HEADS, nq),
        in_specs=[
            pl.BlockSpec((None, tq, LANES), lambda b, h, i: (3 * HEADS + h, b * nq + i, 0)),
            pl.BlockSpec((None, seq, LANES), lambda b, h, i: (4 * HEADS + h, b, 0)),
            pl.BlockSpec((None, seq, LANES), lambda b, h, i: (5 * HEADS + h, b, 0)),
            pl.BlockSpec((1, LANES), lambda b, h, i: (0, 0)),
            pl.BlockSpec((1, LANES), lambda b, h, i: (0, 0)),
            pl.BlockSpec((4, LANES // 2), lambda b, h, i: (0, 0)),
            pl.BlockSpec((1, LANES), lambda b, h, i: (0, 0)),
        ],
        out_specs=pl.BlockSpec((None, tq, LANES), lambda b, h, i: (h, b * nq + i, 0)),
        out_shape=jax.ShapeDtypeStruct((HEADS, n, LANES), BF16),
        scratch_shapes=[
            pltpu.VMEM((seq, LANES), BF16),
            pltpu.VMEM((2 * tq, 1), F32),
            pltpu.VMEM((2 * tq, 1), F32),
            pltpu.VMEM((2 * tq, LANES), F32),
        ],
        compiler_params=pltpu.CompilerParams(
            dimension_semantics=("parallel", "parallel", "arbitrary"),
            vmem_limit_bytes=_vmem_limit(blocks, scratch)),
        name="diffattn",
    )(pb, pb, pb, jnp.tile(qn_w, 2).reshape(1, LANES), jnp.tile(kn_w, 2).reshape(1, LANES),
      lam_p, subln_w.reshape(1, LANES))


def _outproj_kernel(ohg_ref, oda_ref, x_ref, w_ref, o_ref, mix_s):
    for c in range(HEADS):
        mix_s[:, c * LANES:(c + 1) * LANES] = ohg_ref[c]
        mix_s[:, (HEADS + c) * LANES:(HEADS + c + 1) * LANES] = oda_ref[c]
    o_ref[...] = x_ref[...] + _dot(mix_s[...], w_ref[...])


def _outproj(ohg, oda, x, w, *, tm=512):
    n, d = x.shape
    width = w.shape[0]
    tm = min(tm, n)
    blocks = (2 * _nbytes((HEADS, tm, LANES), BF16) + 2 * _nbytes((tm, d), F32) + _nbytes((width, d), BF16))
    return pl.pallas_call(
        _outproj_kernel,
        grid=(n // tm,),
        in_specs=[
            pl.BlockSpec((HEADS, tm, LANES), lambda i: (0, i, 0)),
            pl.BlockSpec((HEADS, tm, LANES), lambda i: (0, i, 0)),
            pl.BlockSpec((tm, d), lambda i: (i, 0)),
            pl.BlockSpec((width, d), lambda i: (0, 0)),
        ],
        out_specs=pl.BlockSpec((tm, d), lambda i: (i, 0)),
        out_shape=jax.ShapeDtypeStruct((n, d), F32),
        scratch_shapes=[pltpu.VMEM((tm, width), BF16)],
        compiler_params=pltpu.CompilerParams(
            dimension_semantics=("parallel",),
            vmem_limit_bytes=_vmem_limit(blocks, _nbytes((tm, width), BF16))),
        name="outproj",
    )(ohg, oda, x, w)


def _ffn_kernel(x_ref, nw_ref, wg_ref, wu_ref, wd_ref, o_ref, h_s):
    @pl.when(pl.program_id(1) == 0)
    def _():
        x = x_ref[...]
        var = jnp.mean(x * x, axis=-1, keepdims=True)
        h_s[...] = (x * lax.rsqrt(var + EPS) * nw_ref[...]).astype(BF16)
        o_ref[...] = x

    h = h_s[...]
    g = _dot(h, wg_ref[...])
    u = _dot(h, wu_ref[...])
    o_ref[...] += _dot((g * _sigmoid(g) * u).astype(BF16), wd_ref[...])


def _ffn(x, nw, wg, wu, wd, *, tm=512, tf=512):
    n, d = x.shape
    ff = wg.shape[1]
    tm = min(tm, n)
    blocks = (2 * _nbytes((tm, d), F32) + 3 * _nbytes((d, tf), BF16))
    scratch = _nbytes((tm, d), BF16) + 3 * _nbytes((tm, tf), F32)
    return pl.pallas_call(
        _ffn_kernel,
        grid=(n // tm, ff // tf),
        in_specs=[
            pl.BlockSpec((tm, d), lambda i, f: (i, 0)),
            pl.BlockSpec((1, d), lambda i, f: (0, 0)),
            pl.BlockSpec((d, tf), lambda i, f: (0, f)),
            pl.BlockSpec((d, tf), lambda i, f: (0, f)),
            pl.BlockSpec((tf, d), lambda i, f: (f, 0)),
        ],
        out_specs=pl.BlockSpec((tm, d), lambda i, f: (i, 0)),
        out_shape=jax.ShapeDtypeStruct((n, d), F32),
        scratch_shapes=[pltpu.VMEM((tm, d), BF16)],
        compiler_params=pltpu.CompilerParams(
            dimension_semantics=("parallel", "arbitrary"),
            vmem_limit_bytes=_vmem_limit(blocks, scratch)),
        name="ffn",
    )(x, nw.reshape(1, d), wg, wu, wd)


def kernel(x, norm_mix_w, w_in, hg_lb_logits, hg_onorm_w, da_qnorm_w, da_knorm_w, da_lambda,
           da_subln_w, w_out, norm_ffn_w, w_gate, w_up, w_down):
    batch, seq, d = x.shape
    depth = w_in.shape[0]
    hg = d // 2
    assert hg == HEADS * LANES and seq % CHUNK == 0
    xf = x.reshape(batch * seq, d).astype(F32)
    lb_logits = hg_lb_logits.astype(F32).reshape(2 * depth, hg)

    for l in range(depth):
        w_z = w_in[l][:, hg:3 * hg].astype(BF16)
        w_r = jnp.concatenate([w_in[l][:, :hg], w_in[l][:, 3 * hg:]], axis=1).astype(BF16)
        pz = _proj(xf, norm_mix_w[l], w_z, F32)
        pb = _proj(xf, norm_mix_w[l], w_r, BF16)
        ohg = _hgrn(pb, pz, lb_logits, hg_onorm_w[l], layer=l, batch=batch, seq=seq)
        oda = _attn(pb, da_qnorm_w[l], da_knorm_w[l], da_lambda[l].astype(F32), da_subln_w[l],
                    layer=l, batch=batch, seq=seq)
        x1 = _outproj(ohg, oda, xf, w_out[l].astype(BF16))
        xf = _ffn(x1, norm_ffn_w[l], w_gate[l].astype(BF16), w_up[l].astype(BF16),
                  w_down[l].astype(BF16))
    return xf.reshape(batch, seq, d).astype(x.dtype)
```

```python
import functools
import math

import jax
import jax.numpy as jnp
from jax import lax
from jax.experimental import pallas as pl
from jax.experimental.pallas import tpu as pltpu

F32 = jnp.float32
BF16 = jnp.bfloat16

LANES = 128
HEADS = 8
EPS = 1e-6
LB_FLOOR = 1e-30
LB_CEIL = 1.0 - 1e-6
CHUNK = 64
SUB = 16
EXP_CLAMP = 80.0
VMEM_CAP = 60 * 1024 * 1024
VMEM_SLACK = 12 * 1024 * 1024


def _vmem_limit(pipelined_bytes, scratch_bytes=0):
    return int(min(VMEM_CAP, 2 * pipelined_bytes + scratch_bytes + VMEM_SLACK))


def _nbytes(shape, dtype):
    return math.prod(shape) * jnp.dtype(dtype).itemsize


def _dot(a, b):
    return jnp.dot(a, b, preferred_element_type=F32)


def _dot_nt(a, b):
    return lax.dot_general(a, b, (((1,), (1,)), ((), ())), preferred_element_type=F32)


def _dot_tn(a, b):
    return lax.dot_general(a, b, (((0,), (0,)), ((), ())), preferred_element_type=F32)


def _sigmoid(x):
    return 1.0 / (1.0 + jnp.exp(-x))


def _proj_kernel(x_ref, nw_ref, w_ref, o_ref, h_s):
    @pl.when(pl.program_id(1) == 0)
    def _():
        x = x_ref[...]
        var = jnp.mean(x * x, axis=-1, keepdims=True)
        h_s[...] = (x * lax.rsqrt(var + EPS) * nw_ref[...]).astype(BF16)

    r = _dot(h_s[...], w_ref[...])
    for c in range(o_ref.shape[0]):
        o_ref[c] = r[:, c * LANES:(c + 1) * LANES].astype(o_ref.dtype)


def _proj(x, nw, w, out_dtype, *, tm=1024, tn=512):
    n, d = x.shape
    cols = w.shape[1]
    tm = min(tm, n)
    grid = (n // tm, cols // tn)
    blocks = (_nbytes((tm, d), F32) + _nbytes((d, tn), BF16) + _nbytes((tm, tn), out_dtype))
    return pl.pallas_call(
        _proj_kernel,
        grid=grid,
        in_specs=[
            pl.BlockSpec((tm, d), lambda i, j: (i, 0)),
            pl.BlockSpec((1, d), lambda i, j: (0, 0)),
            pl.BlockSpec((d, tn), lambda i, j: (0, j)),
        ],
        out_specs=pl.BlockSpec((tn // LANES, tm, LANES), lambda i, j: (j, i, 0)),
        out_shape=jax.ShapeDtypeStruct((cols // LANES, n, LANES), out_dtype),
        scratch_shapes=[pltpu.VMEM((tm, d), BF16)],
        compiler_params=pltpu.CompilerParams(
            dimension_semantics=("parallel", "arbitrary"),
            vmem_limit_bytes=_vmem_limit(blocks, _nbytes((tm, d), BF16) + _nbytes((tm, tn), F32))),
        name="proj",
    )(x, nw.reshape(1, d), w)


def _seg_cumsum(x, seg, rev):
    n = x.shape[0]
    pos = lax.broadcasted_iota(jnp.int32, x.shape, 0) & (seg - 1)
    s = 1
    while s < seg:
        if rev:
            x = x + jnp.where(pos < seg - s, pltpu.roll(x, n - s, 0), 0.0)
        else:
            x = x + jnp.where(pos >= s, pltpu.roll(x, s, 0), 0.0)
        s *= 2
    return x


def _rows_bcast(x, rows, reps):
    return jnp.concatenate([jnp.broadcast_to(x[r:r + 1, :], (reps, x.shape[1])) for r in rows], axis=0)


def _hgrn_kernel(q_ref, v_ref, g_ref, zf_ref, zb_ref, lbl_ref, ow_ref, o_ref,
                 oacc_s, k_s, a_s, v_s, intra_s, *, layer, depth, tblk):
    seq = q_ref.shape[0]
    nblk = seq // tblk
    nchunk = tblk // CHUNK
    nsub = CHUNK // SUB

    def lower_bound(direction):
        lg = lbl_ref[direction * depth:(direction + 1) * depth, :]
        e = jnp.exp(lg - jnp.max(lg, axis=0, keepdims=True))
        p = e / jnp.sum(e, axis=0, keepdims=True)
        lb = jnp.sum(p[:layer + 1], axis=0, keepdims=True) - p[0:1]
        lb = jnp.clip(lb, 0.0, LB_CEIL)
        return lb, jnp.maximum(lb, LB_FLOOR)

    row_c = lax.broadcasted_iota(jnp.int32, (CHUNK, LANES), 0)
    sr = lax.broadcasted_iota(jnp.int32, (CHUNK, CHUNK), 0)
    sc_ = lax.broadcasted_iota(jnp.int32, (CHUNK, CHUNK), 1)

    def run_pass(z_ref, direction, rev, finalize):
        lb, lbp = lower_bound(direction)
        causal = (sc_ >= sr) if rev else (sc_ <= sr)

        def body(i, st):
            blk = (nblk - 1 - i) if rev else i
            r0 = pl.multiple_of(blk * tblk, tblk)
            q = q_ref[pl.ds(r0, tblk), :].astype(F32)
            vb = v_ref[pl.ds(r0, tblk), :]
            z = z_ref[pl.ds(r0, tblk), :]
            sg = _sigmoid(z)
            lf = jnp.log(lbp + (1.0 - lb) * sg)
            k = (1.0 - lb) * (1.0 - sg) - (lbp - lb)
            a = _seg_cumsum(lf, CHUNK, rev)
            ex = a - lf
            edge = (lambda c: c * CHUNK) if rev else (lambda c: c * CHUNK + CHUNK - 1)
            subrow = (lambda j: j * SUB + SUB - 1) if rev else (lambda j: j * SUB)
            a_edge = _rows_bcast(a, [edge(c) for c in range(nchunk)], CHUNK)
            ref_sub = _rows_bcast(ex, [subrow(j) for j in range(tblk // SUB)], SUB)
            d = ref_sub - a
            overflow = jnp.max(d) > EXP_CLAMP
            qs = (q * jnp.exp(-d)).astype(BF16)
            qa = (q * jnp.exp(a)).astype(BF16)
            kb = (k * jnp.exp(a_edge - a)).astype(BF16)

            inter = [None] * nchunk
            intra = [None] * nchunk
            for c in (range(nchunk - 1, -1, -1) if rev else range(nchunk)):
                lo = c * CHUNK
                inter[c] = _dot_nt(qa[lo:lo + CHUNK], st.astype(BF16))
                upd = _dot_tn(vb[lo:lo + CHUNK], kb[lo:lo + CHUNK])
                st = st * jnp.exp(a[edge(c):edge(c) + 1, :]) + upd
                a_c = a[lo:lo + CHUNK]
                k_c = k[lo:lo + CHUNK]
                parts = []
                for j in range(nsub):
                    rr = lo + subrow(j)
                    keep = (row_c >= j * SUB) if rev else (row_c < (j + 1) * SUB)
                    e = jnp.where(keep, jnp.exp(jnp.minimum(ex[rr:rr + 1, :] - a_c, EXP_CLAMP)), 0.0)
                    parts.append(_dot_nt(qs[lo + j * SUB:lo + (j + 1) * SUB], (k_c * e).astype(BF16)))
                scores = jnp.where(causal, jnp.concatenate(parts, axis=0), 0.0).astype(BF16)
                intra[c] = _dot(scores, vb[lo:lo + CHUNK])
            intra_s[...] = jnp.concatenate(intra, axis=0)

            @pl.when(overflow)
            def _():
                k_s[...] = k
                a_s[...] = a
                v_s[...] = vb.astype(F32)
                for c in range(nchunk):
                    lo = c * CHUNK
                    q_c = q[lo:lo + CHUNK]
                    a_c = a[lo:lo + CHUNK]

                    def pair(s, acc):
                        w = q_c * k_s[pl.ds(lo + s, 1), :] * jnp.exp(
                            jnp.minimum(a_c - a_s[pl.ds(lo + s, 1), :], 0.0))
                        r = jnp.sum(w, axis=1, keepdims=True)
                        m = (row_c <= s) if rev else (row_c >= s)
                        return acc + jnp.where(m, r, 0.0) * v_s[pl.ds(lo + s, 1), :]

                    intra_s[lo:lo + CHUNK, :] = lax.fori_loop(
                        0, CHUNK, pair, jnp.zeros((CHUNK, LANES), F32))

            o = jnp.concatenate(inter, axis=0) + intra_s[...]
            if finalize:
                o = o + oacc_s[pl.ds(r0, tblk), :]
                var = jnp.mean(o * o, axis=-1, keepdims=True)
                gate = g_ref[pl.ds(r0, tblk), :].astype(F32)
                o = o * lax.rsqrt(var + EPS) * ow_ref[...] * (gate * _sigmoid(gate))
                o_ref[pl.ds(r0, tblk), :] = o.astype(o_ref.dtype)
            else:
                oacc_s[pl.ds(r0, tblk), :] = o
            return st

        lax.fori_loop(0, nblk, body, jnp.zeros((LANES, LANES), F32))

    run_pass(zf_ref, 0, False, False)
    run_pass(zb_ref, 1, True, True)


def _hgrn(pb, pz, lb_logits, onorm_w, *, layer, batch, seq, tblk=256):
    depth = lb_logits.shape[0] // 2
    n = batch * seq
    tblk = min(tblk, seq)

    def slab(base):
        return pl.BlockSpec((None, seq, LANES), lambda b, h: (base + h, b, 0))

    blocks = 4 * _nbytes((seq, LANES), BF16) + 2 * _nbytes((seq, LANES), F32)
    scratch = _nbytes((seq, LANES), F32) + 4 * _nbytes((tblk, LANES), F32)
    return pl.pallas_call(
        functools.partial(_hgrn_kernel, layer=layer, depth=depth, tblk=tblk),
        grid=(batch, HEADS),
        in_specs=[
            slab(0),
            slab(HEADS),
            slab(2 * HEADS),
            pl.BlockSpec((None, seq, LANES), lambda b, h: (h, b, 0)),
            pl.BlockSpec((None, seq, LANES), lambda b, h: (HEADS + h, b, 0)),
            pl.BlockSpec((2 * depth, LANES), lambda b, h: (0, h)),
            pl.BlockSpec((1, LANES), lambda b, h: (0, 0)),
        ],
        out_specs=pl.BlockSpec((None, seq, LANES), lambda b, h: (h, b, 0)),
        out_shape=jax.ShapeDtypeStruct((HEADS, n, LANES), BF16),
        scratch_shapes=[
            pltpu.VMEM((seq, LANES), F32),
            pltpu.VMEM((tblk, LANES), F32),
            pltpu.VMEM((tblk, LANES), F32),
            pltpu.VMEM((tblk, LANES), F32),
            pltpu.VMEM((tblk, LANES), F32),
        ],
        compiler_params=pltpu.CompilerParams(
            dimension_semantics=("parallel", "parallel"),
            vmem_limit_bytes=_vmem_limit(blocks, scratch)),
        name="hgrn",
    )(pb, pb, pb, pz, pz, lb_logits, onorm_w.reshape(1, LANES))


HALF = LANES // 2
LANE_SHIFT = HALF
LANE_THI = HALF + 1
LANE_TLO = HALF + 2
LANE_SHI = HALF + 3
LANE_SLO = HALF + 4
POS_SPLIT = 256
ROWSUM_MIN = 1e-24


def _half_rms_scale(x, lo_mask):
    sq = x * x
    half = x.shape[1] // 2
    ss_lo = jnp.sum(jnp.where(lo_mask, sq, 0.0), axis=1, keepdims=True)
    ss_hi = jnp.sum(jnp.where(lo_mask, 0.0, sq), axis=1, keepdims=True)
    return jnp.where(lo_mask, lax.rsqrt(ss_lo / half + EPS), lax.rsqrt(ss_hi / half + EPS))


def _split_pos(pos):
    return (pos & ~(POS_SPLIT - 1)).astype(F32), (pos & (POS_SPLIT - 1)).astype(F32)


def _attn_kernel(q_ref, k_ref, v_ref, qw_ref, kw_ref, lam_ref, sw_ref, o_ref,
                 k1_s, k2_s, va_s, kmax_s, q_s, bias_s, acc1_s, acc2_s, m_s, l_s, accx_s,
                 *, tk, lam_init):
    tq = q_ref.shape[0]
    seq = k_ref.shape[0]
    nk = seq // tk
    h = pl.program_id(1)
    qi = pl.program_id(2)
    lane = lax.broadcasted_iota(jnp.int32, (1, LANES), 1)
    lo_mask = lane < HALF
    slope = lax.bitcast_convert_type(jnp.full((1, LANES), (126 - h) << 23, jnp.int32), F32)

    @pl.when(qi == 0)
    def _():
        mx1 = jnp.zeros((1, 1), F32)
        mx2 = jnp.zeros((1, 1), F32)
        for c in range(nk):
            rows = slice(c * tk, (c + 1) * tk)
            kf = k_ref[rows, :].astype(F32)
            kn = kf * _half_rms_scale(kf, lo_mask) * kw_ref[...]
            sq = kn * kn
            mx1 = jnp.maximum(mx1, jnp.max(jnp.sum(jnp.where(lo_mask, sq, 0.0), axis=1, keepdims=True),
                                           axis=0, keepdims=True))
            mx2 = jnp.maximum(mx2, jnp.max(jnp.sum(jnp.where(lo_mask, 0.0, sq), axis=1, keepdims=True),
                                           axis=0, keepdims=True))
            s_hi, s_lo = _split_pos(lax.broadcasted_iota(jnp.int32, (tk, LANES), 0) + c * tk)
            aug = jnp.where(lane == LANE_SHI, slope * s_hi,
                            jnp.where(lane == LANE_SLO, slope * s_lo,
                                      jnp.where((lane >= LANE_SHIFT) & (lane <= LANE_TLO), 1.0, 0.0)))
            k1_s[rows, :] = jnp.where(lo_mask, kn, aug).astype(BF16)
            k2_s[rows, :] = jnp.where(lo_mask, pltpu.roll(kn, HALF, 1), aug).astype(BF16)
            va_s[rows, :LANES] = v_ref[rows, :]
            va_s[rows, LANES:] = jnp.ones((tk, LANES), BF16)
        kmax_s[0:1, :] = jnp.broadcast_to(jnp.sqrt(mx1), (1, LANES))
        kmax_s[1:2, :] = jnp.broadcast_to(jnp.sqrt(mx2), (1, LANES))

    qf = q_ref[...].astype(F32)
    qn = qf * _half_rms_scale(qf, lo_mask) * qw_ref[...] * (HALF ** -0.5)
    qsq = qn * qn
    m1 = jnp.sqrt(jnp.sum(jnp.where(lo_mask, qsq, 0.0), axis=1, keepdims=True)) * kmax_s[0:1, 0:1]
    m2 = jnp.sqrt(jnp.sum(jnp.where(lo_mask, 0.0, qsq), axis=1, keepdims=True)) * kmax_s[1:2, 0:1]
    q1 = jnp.where(lo_mask, qn, 0.0)
    q2 = jnp.where(lo_mask, pltpu.roll(qn, HALF, 1), 0.0)
    t_hi, t_lo = _split_pos(lax.broadcasted_iota(jnp.int32, (tq, LANES), 0) + qi * tq)

    def q_lanes(base, shift, sigma):
        out = jnp.where(lane == LANE_SHIFT, -shift, base)
        if sigma:
            out = jnp.where(lane == LANE_THI, -sigma * slope * t_hi,
                            jnp.where(lane == LANE_TLO, -sigma * slope * t_lo,
                                      jnp.where((lane == LANE_SHI) | (lane == LANE_SLO), sigma, out)))
        return out.astype(BF16)

    for v, sigma in enumerate((1.0, -1.0, 0.0)):
        q_s[v] = q_lanes(q1, m1, sigma)
        q_s[3 + v] = q_lanes(q2, m2, sigma)

    kjm = (qi * tq) // tk
    rel = (lax.broadcasted_iota(jnp.int32, (tq, tk), 0) - lax.broadcasted_iota(jnp.int32, (tq, tk), 1)
           + (qi * tq - kjm * tk)).astype(F32)
    bias_s[0] = jnp.zeros((tq, tk), F32)
    bias_s[1] = jnp.abs(rel) * (-slope[:, :1])
    acc1_s[...] = jnp.zeros(acc1_s.shape, F32)
    acc2_s[...] = jnp.zeros(acc2_s.shape, F32)

    def body(kj, carry):
        k0 = pl.multiple_of(kj * tk, tk)
        diag = kj == kjm
        ver = jnp.where(diag, 2, jnp.where(kj < kjm, 0, 1))
        bias = bias_s[diag.astype(jnp.int32)]
        vv = va_s[pl.ds(k0, tk), :]
        p1 = jnp.exp(_dot_nt(q_s[ver], k1_s[pl.ds(k0, tk), :]) + bias)
        acc1_s[...] += _dot(p1.astype(BF16), vv)
        p2 = jnp.exp(_dot_nt(q_s[3 + ver], k2_s[pl.ds(k0, tk), :]) + bias)
        acc2_s[...] += _dot(p2.astype(BF16), vv)
        return carry

    lax.fori_loop(0, nk, body, 0, unroll=True)

    lam_p = lam_ref[...]
    lam = (jnp.exp(jnp.sum(lam_p[0:1] * lam_p[1:2], axis=1, keepdims=True))
           - jnp.exp(jnp.sum(lam_p[2:3] * lam_p[3:4], axis=1, keepdims=True)) + lam_init)

    def finish(o1, o2):
        o = o1 - lam * o2
        var = jnp.mean(o * o, axis=-1, keepdims=True)
        o = o * lax.rsqrt(var + EPS) * sw_ref[...] * (1.0 - lam_init)
        o_ref[...] = o.astype(o_ref.dtype)

    a1 = acc1_s[...]
    a2 = acc2_s[...]
    l1 = a1[:, LANES:LANES + 1]
    l2 = a2[:, LANES:LANES + 1]
    finish(a1[:, :LANES] / l1, a2[:, :LANES] / l2)

    @pl.when(jnp.minimum(jnp.min(l1), jnp.min(l2)) < ROWSUM_MIN)
    def _():
        q_both = jnp.concatenate([q1, q2], axis=0).astype(BF16)
        m_s[...] = jnp.full(m_s.shape, -jnp.inf, F32)
        l_s[...] = jnp.zeros(l_s.shape, F32)
        accx_s[...] = jnp.zeros(accx_s.shape, F32)

        def exact_body(kj, carry):
            k0 = pl.multiple_of(kj * tk, tk)
            s = jnp.concatenate([_dot_nt(q_both[:tq], k1_s[pl.ds(k0, tk), :]),
                                 _dot_nt(q_both[tq:], k2_s[pl.ds(k0, tk), :])], axis=0)
            pen = jnp.abs(rel + ((kjm - kj) * tk).astype(F32)) * (-slope[:, :1])
            s = s + jnp.concatenate([pen, pen], axis=0)
            m_prev = m_s[...]
            m_new = jnp.maximum(m_prev, jnp.max(s, axis=1, keepdims=True))
            alpha = jnp.exp(m_prev - m_new)
            p = jnp.exp(s - m_new)
            l_s[...] = alpha * l_s[...] + jnp.sum(p, axis=1, keepdims=True)
            accx_s[...] = alpha * accx_s[...] + _dot(p.astype(BF16), va_s[pl.ds(k0, tk), :LANES])
            m_s[...] = m_new
            return carry

        lax.fori_loop(0, nk, exact_body, 0)
        acc = accx_s[...]
        l = l_s[...]
        finish(acc[:tq] / l[:tq], acc[tq:] / l[tq:])


def _attn(pb, qn_w, kn_w, lam_p, subln_w, *, layer, batch, seq, tq=256, tk=512):
    n = batch * seq
    tq = min(tq, seq)
    tk = min(tk, seq)
    nq = seq // tq
    assert tk % tq == 0 and seq % tk == 0 and seq <= POS_SPLIT * POS_SPLIT
    lam_init = 0.8 - 0.6 * math.exp(-0.3 * layer)
    blocks = 2 * _nbytes((tq, LANES), BF16) + 2 * _nbytes((seq, LANES), BF16)
    scratch = (4 * _nbytes((seq, LANES), BF16) + 6 * _nbytes((tq, LANES), BF16)
               + 2 * _nbytes((tq, tk), F32) + 2 * _nbytes((tq, 2 * LANES), F32)
               + 3 * _nbytes((2 * tq, LANES), F32) + 4 * _nbytes((2 * tq, tk), F32))
    return pl.pallas_call(
        functools.partial(_attn_kernel, tk=tk, lam_init=lam_init),
        grid=(batch, HEADS, nq),
        in_specs=[
            pl.BlockSpec((None, tq, LANES), lambda b, h, i: (3 * HEADS + h, b * nq + i, 0)),
            pl.BlockSpec((None, seq, LANES), lambda b, h, i: (4 * HEADS + h, b, 0)),
            pl.BlockSpec((None, seq, LANES), lambda b, h, i: (5 * HEADS + h, b, 0)),
            pl.BlockSpec((1, LANES), lambda b, h, i: (0, 0)),
            pl.BlockSpec((1, LANES), lambda b, h, i: (0, 0)),
            pl.BlockSpec((4, LANES // 2), lambda b, h, i: (0, 0)),
            pl.BlockSpec((1, LANES), lambda b, h, i: (0, 0)),
        ],
        out_specs=pl.BlockSpec((None, tq, LANES), lambda b, h, i: (h, b * nq + i, 0)),
        out_shape=jax.ShapeDtypeStruct((HEADS, n, LANES), BF16),
        scratch_shapes=[
            pltpu.VMEM((seq, LANES), BF16),
            pltpu.VMEM((seq, LANES), BF16),
            pltpu.VMEM((seq, 2 * LANES), BF16),
            pltpu.VMEM((8, LANES), F32),
            pltpu.VMEM((6, tq, LANES), BF16),
            pltpu.VMEM((2, tq, tk), F32),
            pltpu.VMEM((tq, 2 * LANES), F32),
            pltpu.VMEM((tq, 2 * LANES), F32),
            pltpu.VMEM((2 * tq, 1), F32),
            pltpu.VMEM((2 * tq, 1), F32),
            pltpu.VMEM((2 * tq, LANES), F32),
        ],
        compiler_params=pltpu.CompilerParams(
            dimension_semantics=("parallel", "parallel", "arbitrary"),
            vmem_limit_bytes=_vmem_limit(blocks, scratch)),
        name="diffattn",
    )(pb, pb, pb, jnp.tile(qn_w, 2).reshape(1, LANES), jnp.tile(kn_w, 2).reshape(1, LANES),
      lam_p, subln_w.reshape(1, LANES))


def _outproj_kernel(ohg_ref, oda_ref, x_ref, w_ref, o_ref, mix_s):
    for c in range(HEADS):
        mix_s[:, c * LANES:(c + 1) * LANES] = ohg_ref[c]
        mix_s[:, (HEADS + c) * LANES:(HEADS + c + 1) * LANES] = oda_ref[c]
    o_ref[...] = x_ref[...] + _dot(mix_s[...], w_ref[...])


def _outproj(ohg, oda, x, w, *, tm=512):
    n, d = x.shape
    width = w.shape[0]
    tm = min(tm, n)
    blocks = (2 * _nbytes((HEADS, tm, LANES), BF16) + 2 * _nbytes((tm, d), F32) + _nbytes((width, d), BF16))
    return pl.pallas_call(
        _outproj_kernel,
        grid=(n // tm,),
        in_specs=[
            pl.BlockSpec((HEADS, tm, LANES), lambda i: (0, i, 0)),
            pl.BlockSpec((HEADS, tm, LANES), lambda i: (0, i, 0)),
            pl.BlockSpec((tm, d), lambda i: (i, 0)),
            pl.BlockSpec((width, d), lambda i: (0, 0)),
        ],
        out_specs=pl.BlockSpec((tm, d), lambda i: (i, 0)),
        out_shape=jax.ShapeDtypeStruct((n, d), F32),
        scratch_shapes=[pltpu.VMEM((tm, width), BF16)],
        compiler_params=pltpu.CompilerParams(
            dimension_semantics=("parallel",),
            vmem_limit_bytes=_vmem_limit(blocks, _nbytes((tm, width), BF16))),
        name="outproj",
    )(ohg, oda, x, w)


def _ffn_kernel(x_ref, nw_ref, wg_ref, wu_ref, wd_ref, o_ref, h_s):
    @pl.when(pl.program_id(1) == 0)
    def _():
        x = x_ref[...]
        var = jnp.mean(x * x, axis=-1, keepdims=True)
        h_s[...] = (x * lax.rsqrt(var + EPS) * nw_ref[...]).astype(BF16)
        o_ref[...] = x

    h = h_s[...]
    g = _dot(h, wg_ref[...])
    u = _dot(h, wu_ref[...])
    o_ref[...] += _dot((g * _sigmoid(g) * u).astype(BF16), wd_ref[...])


def _ffn(x, nw, wg, wu, wd, *, tm=512, tf=512):
    n, d = x.shape
    ff = wg.shape[1]
    tm = min(tm, n)
    blocks = (2 * _nbytes((tm, d), F32) + 3 * _nbytes((d, tf), BF16))
    scratch = _nbytes((tm, d), BF16) + 3 * _nbytes((tm, tf), F32)
    return pl.pallas_call(
        _ffn_kernel,
        grid=(n // tm, ff // tf),
        in_specs=[
            pl.BlockSpec((tm, d), lambda i, f: (i, 0)),
            pl.BlockSpec((1, d), lambda i, f: (0, 0)),
            pl.BlockSpec((d, tf), lambda i, f: (0, f)),
            pl.BlockSpec((d, tf), lambda i, f: (0, f)),
            pl.BlockSpec((tf, d), lambda i, f: (f, 0)),
        ],
        out_specs=pl.BlockSpec((tm, d), lambda i, f: (i, 0)),
        out_shape=jax.ShapeDtypeStruct((n, d), F32),
        scratch_shapes=[pltpu.VMEM((tm, d), BF16)],
        compiler_params=pltpu.CompilerParams(
            dimension_semantics=("parallel", "arbitrary"),
            vmem_limit_bytes=_vmem_limit(blocks, scratch)),
        name="ffn",
    )(x, nw.reshape(1, d), wg, wu, wd)


def kernel(x, norm_mix_w, w_in, hg_lb_logits, hg_onorm_w, da_qnorm_w, da_knorm_w, da_lambda,
           da_subln_w, w_out, norm_ffn_w, w_gate, w_up, w_down):
    batch, seq, d = x.shape
    depth = w_in.shape[0]
    hg = d // 2
    assert hg == HEADS * LANES and seq % CHUNK == 0
    xf = x.reshape(batch * seq, d).astype(F32)
    lb_logits = hg_lb_logits.astype(F32).reshape(2 * depth, hg)

    for l in range(depth):
        w_z = w_in[l][:, hg:3 * hg].astype(BF16)
        w_r = jnp.concatenate([w_in[l][:, :hg], w_in[l][:, 3 * hg:]], axis=1).astype(BF16)
        pz = _proj(xf, norm_mix_w[l], w_z, F32)
        pb = _proj(xf, norm_mix_w[l], w_r, BF16)
        ohg = _hgrn(pb, pz, lb_logits, hg_onorm_w[l], layer=l, batch=batch, seq=seq)
        oda = _attn(pb, da_qnorm_w[l], da_knorm_w[l], da_lambda[l].astype(F32), da_subln_w[l],
                    layer=l, batch=batch, seq=seq)
        x1 = _outproj(ohg, oda, xf, w_out[l].astype(BF16))
        xf = _ffn(x1, norm_ffn_w[l], w_gate[l].astype(BF16), w_up[l].astype(BF16),
                  w_down[l].astype(BF16))
    return xf.reshape(batch, seq, d).astype(x.dtype)
```

```python
import functools
import math

import jax
import jax.numpy as jnp
from jax import lax
from jax.experimental import pallas as pl
from jax.experimental.pallas import tpu as pltpu

F32 = jnp.float32
BF16 = jnp.bfloat16

LANES = 128
HEADS = 8
EPS = 1e-6
LB_FLOOR = 1e-30
LB_CEIL = 1.0 - 1e-6
CHUNK = 64
EXP_CLAMP = 80.0
VMEM_CAP = 60 * 1024 * 1024
VMEM_SLACK = 12 * 1024 * 1024


def _vmem_limit(pipelined_bytes, scratch_bytes=0):
    return int(min(VMEM_CAP, 2 * pipelined_bytes + scratch_bytes + VMEM_SLACK))


def _nbytes(shape, dtype):
    return math.prod(shape) * jnp.dtype(dtype).itemsize


def _dot(a, b):
    return jnp.dot(a, b, preferred_element_type=F32)


def _dot_nt(a, b):
    return lax.dot_general(a, b, (((1,), (1,)), ((), ())), preferred_element_type=F32)


def _sigmoid(x):
    return 1.0 / (1.0 + jnp.exp(-x))


def _proj_kernel(x_ref, nw_ref, w_ref, o_ref, h_s, *, transposed):
    @pl.when(pl.program_id(1) == 0)
    def _():
        x = x_ref[...]
        var = jnp.mean(x * x, axis=-1, keepdims=True)
        h_s[...] = (x * lax.rsqrt(var + EPS) * nw_ref[...]).astype(BF16)

    if transposed:
        r = _dot_nt(w_ref[...], h_s[...])
        for c in range(o_ref.shape[0]):
            o_ref[c] = r[c * LANES:(c + 1) * LANES, :].astype(o_ref.dtype)
    else:
        r = _dot(h_s[...], w_ref[...])
        for c in range(o_ref.shape[0]):
            o_ref[c] = r[:, c * LANES:(c + 1) * LANES].astype(o_ref.dtype)


def _proj(x, nw, w, out_dtype, *, transposed=False, tm=1024, tn=512):
    n, d = x.shape
    cols = w.shape[0] if transposed else w.shape[1]
    tm = min(tm, n)
    grid = (n // tm, cols // tn)
    blocks = (_nbytes((tm, d), F32) + _nbytes((d, tn), BF16) + _nbytes((tm, tn), out_dtype))
    if transposed:
        w_spec = pl.BlockSpec((tn, d), lambda i, j: (j, 0))
        out_spec = pl.BlockSpec((tn // LANES, LANES, tm), lambda i, j: (j, 0, i))
        out_shape = (cols // LANES, LANES, n)
    else:
        w_spec = pl.BlockSpec((d, tn), lambda i, j: (0, j))
        out_spec = pl.BlockSpec((tn // LANES, tm, LANES), lambda i, j: (j, i, 0))
        out_shape = (cols // LANES, n, LANES)
    return pl.pallas_call(
        functools.partial(_proj_kernel, transposed=transposed),
        grid=grid,
        in_specs=[
            pl.BlockSpec((tm, d), lambda i, j: (i, 0)),
            pl.BlockSpec((1, d), lambda i, j: (0, 0)),
            w_spec,
        ],
        out_specs=out_spec,
        out_shape=jax.ShapeDtypeStruct(out_shape, out_dtype),
        scratch_shapes=[pltpu.VMEM((tm, d), BF16)],
        compiler_params=pltpu.CompilerParams(
            dimension_semantics=("parallel", "arbitrary"),
            vmem_limit_bytes=_vmem_limit(blocks, _nbytes((tm, d), BF16) + _nbytes((tm, tn), F32))),
        name="proj",
    )(x, nw.reshape(1, d), w)


def _seg_cumsum(x, seg, rev):
    n = x.shape[0]
    pos = lax.broadcasted_iota(jnp.int32, x.shape, 0) & (seg - 1)
    s = 1
    while s < seg:
        if rev:
            x = x + jnp.where(pos < seg - s, pltpu.roll(x, n - s, 0), 0.0)
        else:
            x = x + jnp.where(pos >= s, pltpu.roll(x, s, 0), 0.0)
        s *= 2
    return x


def _rows_bcast(x, rows, reps):
    return jnp.concatenate([jnp.broadcast_to(x[r:r + 1, :], (reps, x.shape[1])) for r in rows], axis=0)


def _hgrn_kernel(q_ref, vt_ref, g_ref, zf_ref, zb_ref, lbl_ref, ow_ref, o_ref,
                 oacc_s, k_s, a_s, v_s, intra_s, *, layer, depth, tblk):
    seq = q_ref.shape[0]
    nblk = seq // tblk
    nchunk = tblk // CHUNK

    def lower_bound(direction):
        lg = lbl_ref[direction * depth:(direction + 1) * depth, :]
        e = jnp.exp(lg - jnp.max(lg, axis=0, keepdims=True))
        p = e / jnp.sum(e, axis=0, keepdims=True)
        lb = jnp.sum(p[:layer + 1], axis=0, keepdims=True) - p[0:1]
        lb = jnp.clip(lb, 0.0, LB_CEIL)
        return lb, jnp.maximum(lb, LB_FLOOR)

    row_c = lax.broadcasted_iota(jnp.int32, (CHUNK, LANES), 0)
    sr = lax.broadcasted_iota(jnp.int32, (tblk, tblk), 0)
    sc_ = lax.broadcasted_iota(jnp.int32, (tblk, tblk), 1)
    same_chunk = (sr // CHUNK) == (sc_ // CHUNK)
    own_lanes = (lax.broadcasted_iota(jnp.int32, (tblk, nchunk * LANES), 0) // CHUNK
                 == lax.broadcasted_iota(jnp.int32, (tblk, nchunk * LANES), 1) // LANES)

    bounds = (lower_bound(0), lower_bound(1))
    z_refs = (zf_ref, zb_ref)

    def fast_block(direction, r0, st):
        rev = direction == 1
        lb, lbp = bounds[direction]
        causal = same_chunk & ((sc_ >= sr) if rev else (sc_ <= sr))
        q = q_ref[pl.ds(r0, tblk), :].astype(F32)
        vt = vt_ref[:, pl.ds(r0, tblk)]
        z = z_refs[direction][pl.ds(r0, tblk), :]
        sg = _sigmoid(z)
        lf = jnp.log(lbp + (1.0 - lb) * sg)
        k = (1.0 - lb) * (1.0 - sg) - (lbp - lb)
        a = _seg_cumsum(lf, CHUNK, rev)
        edge = (lambda c: c * CHUNK) if rev else (lambda c: c * CHUNK + CHUNK - 1)
        a_edge = _rows_bcast(a, [edge(c) for c in range(nchunk)], CHUNK)
        d = a - _rows_bcast(a, [c * CHUNK + CHUNK // 2 for c in range(nchunk)], CHUNK)
        overflow = jnp.max(jnp.abs(d)) > EXP_CLAMP
        e_mid = jnp.exp(jnp.clip(d, -EXP_CLAMP, EXP_CLAMP))
        qs = (q * e_mid).astype(BF16)
        ks = (k / e_mid).astype(BF16)
        qa = (q * jnp.exp(a)).astype(BF16)
        kb = (k * jnp.exp(a_edge - a)).astype(BF16)
        k_s[direction] = k
        a_s[direction] = a

        scores = jnp.where(causal, _dot_nt(qs, ks), 0.0).astype(BF16)
        intra_s[direction] = _dot_nt(scores, vt)
        upd = _dot(vt, jnp.where(own_lanes, jnp.concatenate([kb] * nchunk, axis=1), 0.0))
        inter = [None] * nchunk
        for c in (range(nchunk - 1, -1, -1) if rev else range(nchunk)):
            lo = c * CHUNK
            inter[c] = _dot_nt(qa[lo:lo + CHUNK], st.astype(BF16))
            st = st * jnp.exp(a[edge(c):edge(c) + 1, :]) + upd[:, c * LANES:(c + 1) * LANES]
        return jnp.concatenate(inter, axis=0), overflow, st

    def exact_intra(direction, r0):
        rev = direction == 1
        v_s[...] = vt_ref[:, pl.ds(r0, tblk)].astype(F32).T
        for c in range(nchunk):
            lo = c * CHUNK
            q_c = q_ref[pl.ds(r0 + lo, CHUNK), :].astype(F32)
            a_c = a_s[direction, lo:lo + CHUNK, :]

            def pair(s, acc):
                w = q_c * k_s[direction, pl.ds(lo + s, 1), :] * jnp.exp(
                    jnp.minimum(a_c - a_s[direction, pl.ds(lo + s, 1), :], 0.0))
                r = jnp.sum(w, axis=1, keepdims=True)
                m = (row_c <= s) if rev else (row_c >= s)
                return acc + jnp.where(m, r, 0.0) * v_s[pl.ds(lo + s, 1), :]

            intra_s[direction, lo:lo + CHUNK, :] = lax.fori_loop(
                0, CHUNK, pair, jnp.zeros((CHUNK, LANES), F32))

    def emit(r0, o):
        var = jnp.mean(o * o, axis=-1, keepdims=True)
        gate = g_ref[pl.ds(r0, tblk), :].astype(F32)
        o = o * lax.rsqrt(var + EPS) * ow_ref[...] * (gate * _sigmoid(gate))
        o_ref[pl.ds(r0, tblk), :] = o.astype(o_ref.dtype)

    def sweep(second_half):
        def body(i, carry):
            st_f, st_b = carry
            rf = pl.multiple_of(i * tblk, tblk)
            rb = pl.multiple_of((nblk - 1 - i) * tblk, tblk)
            inter_f, ovf_f, st_f = fast_block(0, rf, st_f)
            inter_b, ovf_b, st_b = fast_block(1, rb, st_b)

            @pl.when(ovf_f)
            def _():
                exact_intra(0, rf)

            @pl.when(ovf_b)
            def _():
                exact_intra(1, rb)

            o_f = inter_f + intra_s[0]
            o_b = inter_b + intra_s[1]
            if second_half:
                emit(rf, o_f + oacc_s[pl.ds(rf, tblk), :])
                emit(rb, o_b + oacc_s[pl.ds(rb, tblk), :])
            else:
                oacc_s[pl.ds(rf, tblk), :] = o_f
                oacc_s[pl.ds(rb, tblk), :] = o_b
            return st_f, st_b
        return body

    zero = jnp.zeros((LANES, LANES), F32)
    carry = lax.fori_loop(0, nblk // 2, sweep(False), (zero, zero))
    lax.fori_loop(nblk // 2, nblk, sweep(True), carry)


def _hgrn(pb, pvt, pz, lb_logits, onorm_w, *, layer, batch, seq, tblk=256):
    depth = lb_logits.shape[0] // 2
    n = batch * seq
    tblk = min(tblk, seq)

    def slab(base):
        return pl.BlockSpec((None, seq, LANES), lambda b, h: (base + h, b, 0))

    assert seq % (2 * tblk) == 0
    blocks = 4 * _nbytes((seq, LANES), BF16) + 2 * _nbytes((seq, LANES), F32)
    scratch = _nbytes((seq, LANES), F32) + 7 * _nbytes((tblk, LANES), F32)
    return pl.pallas_call(
        functools.partial(_hgrn_kernel, layer=layer, depth=depth, tblk=tblk),
        grid=(batch, HEADS),
        in_specs=[
            slab(0),
            pl.BlockSpec((None, LANES, seq), lambda b, h: (h, 0, b)),
            slab(HEADS),
            pl.BlockSpec((None, seq, LANES), lambda b, h: (h, b, 0)),
            pl.BlockSpec((None, seq, LANES), lambda b, h: (HEADS + h, b, 0)),
            pl.BlockSpec((2 * depth, LANES), lambda b, h: (0, h)),
            pl.BlockSpec((1, LANES), lambda b, h: (0, 0)),
        ],
        out_specs=pl.BlockSpec((None, seq, LANES), lambda b, h: (h, b, 0)),
        out_shape=jax.ShapeDtypeStruct((HEADS, n, LANES), BF16),
        scratch_shapes=[
            pltpu.VMEM((seq, LANES), F32),
            pltpu.VMEM((2, tblk, LANES), F32),
            pltpu.VMEM((2, tblk, LANES), F32),
            pltpu.VMEM((tblk, LANES), F32),
            pltpu.VMEM((2, tblk, LANES), F32),
        ],
        compiler_params=pltpu.CompilerParams(
            dimension_semantics=("parallel", "parallel"),
            vmem_limit_bytes=_vmem_limit(blocks, scratch)),
        name="hgrn",
    )(pb, pvt, pb, pz, pz, lb_logits, onorm_w.reshape(1, LANES))


HALF = LANES // 2
LANE_SHIFT = HALF
LANE_THI = HALF + 1
LANE_TLO = HALF + 2
LANE_SHI = HALF + 3
LANE_SLO = HALF + 4
POS_SPLIT = 256
ROWSUM_MIN = 1e-24


def _half_rms_scale(x, lo_mask):
    sq = x * x
    half = x.shape[1] // 2
    ss_lo = jnp.sum(jnp.where(lo_mask, sq, 0.0), axis=1, keepdims=True)
    ss_hi = jnp.sum(jnp.where(lo_mask, 0.0, sq), axis=1, keepdims=True)
    return jnp.where(lo_mask, lax.rsqrt(ss_lo / half + EPS), lax.rsqrt(ss_hi / half + EPS))


def _split_pos(pos):
    return (pos & ~(POS_SPLIT - 1)).astype(F32), (pos & (POS_SPLIT - 1)).astype(F32)


def _attn_kernel(q_ref, k_ref, v_ref, qw_ref, kw_ref, lam_ref, sw_ref, o_ref,
                 k1_s, k2_s, va_s, kpos_s, qpos_s, dist_s, q_s, bias_s, acc1_s, acc2_s, m_s, l_s, accx_s,
                 *, tk, lam_init):
    tq = q_ref.shape[0]
    seq = k_ref.shape[0]
    nk = seq // tk
    nvar = tk // tq
    h = pl.program_id(1)
    qi = pl.program_id(2)
    lane = lax.broadcasted_iota(jnp.int32, (1, LANES), 1)
    lo_mask = lane < HALF
    slope = lax.bitcast_convert_type(jnp.full((1, LANES), (126 - h) << 23, jnp.int32), F32)

    @pl.when((pl.program_id(0) == 0) & (h == 0) & (qi == 0))
    def _():
        hi, lo = _split_pos(lax.broadcasted_iota(jnp.int32, (seq, LANES), 0))
        kpos_s[...] = jnp.where(lane == LANE_SHI, hi, jnp.where(
            lane == LANE_SLO, lo, jnp.where((lane >= LANE_SHIFT) & (lane <= LANE_TLO), 1.0, 0.0)))
        qpos_s[...] = jnp.where(lane == LANE_THI, -hi, jnp.where(
            lane == LANE_TLO, -lo, jnp.where((lane == LANE_SHI) | (lane == LANE_SLO), 1.0, 0.0)))
        rel = (lax.broadcasted_iota(jnp.int32, (tq, tk), 0) - lax.broadcasted_iota(jnp.int32, (tq, tk), 1))
        for v in range(nvar):
            dist_s[v] = jnp.abs(rel + v * tq).astype(F32)
        bias_s[0] = jnp.zeros((tq, tk), F32)

    @pl.when(qi == 0)
    def _():
        for c in range(nk):
            rows = slice(c * tk, (c + 1) * tk)
            kf = k_ref[rows, :].astype(F32)
            kn = kf * _half_rms_scale(kf, lo_mask) * kw_ref[...]
            pos = kpos_s[rows, :]
            k1_s[rows, :] = jnp.where(lo_mask, kn, pos).astype(BF16)
            k2_s[rows, :] = jnp.where(lo_mask, pltpu.roll(kn, HALF, 1), pos).astype(BF16)
            va_s[rows, :LANES] = v_ref[rows, :]
            va_s[rows, LANES:] = jnp.ones((tk, LANES), BF16)
        for v in range(nvar):
            bias_s[1 + v] = dist_s[v] * (-slope[:, :1])

    def gain_max(w_ref, mask):
        return jnp.max(jnp.where(mask, jnp.abs(w_ref[...]), 0.0), axis=1, keepdims=True)

    m1 = math.sqrt(HALF) * gain_max(qw_ref, lo_mask) * gain_max(kw_ref, lo_mask)
    m2 = math.sqrt(HALF) * gain_max(qw_ref, ~lo_mask) * gain_max(kw_ref, ~lo_mask)

    qf = q_ref[...].astype(F32)
    qn = qf * _half_rms_scale(qf, lo_mask) * qw_ref[...] * (HALF ** -0.5)
    q1 = jnp.where(lo_mask, qn, 0.0)
    q2 = jnp.where(lo_mask, pltpu.roll(qn, HALF, 1), 0.0)
    q1m = jnp.where(lane == LANE_SHIFT, -m1, q1)
    q2m = jnp.where(lane == LANE_SHIFT, -m2, q2)
    pen = qpos_s[pl.ds(pl.multiple_of(qi * tq, tq), tq), :] * slope
    for v, sigma in enumerate((1.0, -1.0, 0.0)):
        q_s[v] = (q1m + sigma * pen).astype(BF16)
        q_s[3 + v] = (q2m + sigma * pen).astype(BF16)

    kjm = (qi * tq) // tk
    diag_tile = 1 + (qi * tq - kjm * tk) // tq

    for kj in range(nk):
        rows = slice(kj * tk, (kj + 1) * tk)
        diag = kjm == kj
        ver = jnp.where(diag, 2, jnp.where(kjm > kj, 0, 1))
        bias = bias_s[jnp.where(diag, diag_tile, 0)]
        vv = va_s[rows, :]
        p1 = jnp.exp(_dot_nt(q_s[ver], k1_s[rows, :]) + bias)
        p2 = jnp.exp(_dot_nt(q_s[3 + ver], k2_s[rows, :]) + bias)
        if kj == 0:
            acc1_s[...] = _dot(p1.astype(BF16), vv)
            acc2_s[...] = _dot(p2.astype(BF16), vv)
        else:
            acc1_s[...] += _dot(p1.astype(BF16), vv)
            acc2_s[...] += _dot(p2.astype(BF16), vv)

    lam_p = lam_ref[...]
    lam = (jnp.exp(jnp.sum(lam_p[0:1] * lam_p[1:2], axis=1, keepdims=True))
           - jnp.exp(jnp.sum(lam_p[2:3] * lam_p[3:4], axis=1, keepdims=True)) + lam_init)

    def finish(o1, o2):
        o = o1 - lam * o2
        var = jnp.mean(o * o, axis=-1, keepdims=True)
        o = o * lax.rsqrt(var + EPS) * sw_ref[...] * (1.0 - lam_init)
        o_ref[...] = o.astype(o_ref.dtype)

    a1 = acc1_s[...]
    a2 = acc2_s[...]
    l1 = a1[:, LANES:]
    l2 = a2[:, LANES:]
    finish(a1[:, :LANES] / l1, a2[:, :LANES] / l2)

    @pl.when(jnp.minimum(jnp.min(l1), jnp.min(l2)) < ROWSUM_MIN)
    def _():
        q_both = jnp.concatenate([q1, q2], axis=0).astype(BF16)
        m_s[...] = jnp.full(m_s.shape, -jnp.inf, F32)
        l_s[...] = jnp.zeros(l_s.shape, F32)
        accx_s[...] = jnp.zeros(accx_s.shape, F32)
        rel = (lax.broadcasted_iota(jnp.int32, (tq, tk), 0) - lax.broadcasted_iota(jnp.int32, (tq, tk), 1)
               + qi * tq).astype(F32)

        def exact_body(kj, carry):
            k0 = pl.multiple_of(kj * tk, tk)
            s = jnp.concatenate([_dot_nt(q_both[:tq], k1_s[pl.ds(k0, tk), :]),
                                 _dot_nt(q_both[tq:], k2_s[pl.ds(k0, tk), :])], axis=0)
            pen = jnp.abs(rel - (kj * tk).astype(F32)) * (-slope[:, :1])
            s = s + jnp.concatenate([pen, pen], axis=0)
            m_prev = m_s[...]
            m_new = jnp.maximum(m_prev, jnp.max(s, axis=1, keepdims=True))
            alpha = jnp.exp(m_prev - m_new)
            p = jnp.exp(s - m_new)
            l_s[...] = alpha * l_s[...] + jnp.sum(p, axis=1, keepdims=True)
            accx_s[...] = alpha * accx_s[...] + _dot(p.astype(BF16), va_s[pl.ds(k0, tk), :LANES])
            m_s[...] = m_new
            return carry

        lax.fori_loop(0, nk, exact_body, 0)
        acc = accx_s[...]
        l = l_s[...]
        finish(acc[:tq] / l[:tq], acc[tq:] / l[tq:])


def _attn(pb, qn_w, kn_w, lam_p, subln_w, *, layer, batch, seq, tq=256, tk=512):
    n = batch * seq
    tq = min(tq, seq)
    tk = min(tk, seq)
    nq = seq // tq
    assert tk % tq == 0 and seq % tk == 0 and seq <= POS_SPLIT * POS_SPLIT
    lam_init = 0.8 - 0.6 * math.exp(-0.3 * layer)
    blocks = 2 * _nbytes((tq, LANES), BF16) + 2 * _nbytes((seq, LANES), BF16)
    scratch = (4 * _nbytes((seq, LANES), BF16) + 2 * _nbytes((seq, LANES), F32)
               + 6 * _nbytes((tq, LANES), BF16)
               + (1 + 2 * (tk // tq)) * _nbytes((tq, tk), F32) + 2 * _nbytes((tq, 2 * LANES), F32)
               + 3 * _nbytes((2 * tq, LANES), F32) + 4 * _nbytes((2 * tq, tk), F32))
    return pl.pallas_call(
        functools.partial(_attn_kernel, tk=tk, lam_init=lam_init),
        grid=(batch, HEADS, nq),
        in_specs=[
            pl.BlockSpec((None, tq, LANES), lambda b, h, i: (2 * HEADS + h, b * nq + i, 0)),
            pl.BlockSpec((None, seq, LANES), lambda b, h, i: (3 * HEADS + h, b, 0)),
            pl.BlockSpec((None, seq, LANES), lambda b, h, i: (4 * HEADS + h, b, 0)),
            pl.BlockSpec((1, LANES), lambda b, h, i: (0, 0)),
            pl.BlockSpec((1, LANES), lambda b, h, i: (0, 0)),
            pl.BlockSpec((4, LANES // 2), lambda b, h, i: (0, 0)),
            pl.BlockSpec((1, LANES), lambda b, h, i: (0, 0)),
        ],
        out_specs=pl.BlockSpec((None, tq, LANES), lambda b, h, i: (h, b * nq + i, 0)),
        out_shape=jax.ShapeDtypeStruct((HEADS, n, LANES), BF16),
        scratch_shapes=[
            pltpu.VMEM((seq, LANES), BF16),
            pltpu.VMEM((seq, LANES), BF16),
            pltpu.VMEM((seq, 2 * LANES), BF16),
            pltpu.VMEM((seq, LANES), F32),
            pltpu.VMEM((seq, LANES), F32),
            pltpu.VMEM((tk // tq, tq, tk), F32),
            pltpu.VMEM((6, tq, LANES), BF16),
            pltpu.VMEM((1 + tk // tq, tq, tk), F32),
            pltpu.VMEM((tq, 2 * LANES), F32),
            pltpu.VMEM((tq, 2 * LANES), F32),
            pltpu.VMEM((2 * tq, 1), F32),
            pltpu.VMEM((2 * tq, 1), F32),
            pltpu.VMEM((2 * tq, LANES), F32),
        ],
        compiler_params=pltpu.CompilerParams(
            dimension_semantics=("arbitrary", "arbitrary", "arbitrary"),
            vmem_limit_bytes=_vmem_limit(blocks, scratch)),
        name="diffattn",
    )(pb, pb, pb, jnp.tile(qn_w, 2).reshape(1, LANES), jnp.tile(kn_w, 2).reshape(1, LANES),
      lam_p, subln_w.reshape(1, LANES))


def _outproj_kernel(ohg_ref, oda_ref, x_ref, w_ref, o_ref, mix_s):
    for c in range(HEADS):
        mix_s[:, c * LANES:(c + 1) * LANES] = ohg_ref[c]
        mix_s[:, (HEADS + c) * LANES:(HEADS + c + 1) * LANES] = oda_ref[c]
    o_ref[...] = x_ref[...] + _dot(mix_s[...], w_ref[...])


def _outproj(ohg, oda, x, w, *, tm=512):
    n, d = x.shape
    width = w.shape[0]
    tm = min(tm, n)
    blocks = (2 * _nbytes((HEADS, tm, LANES), BF16) + 2 * _nbytes((tm, d), F32) + _nbytes((width, d), BF16))
    return pl.pallas_call(
        _outproj_kernel,
        grid=(n // tm,),
        in_specs=[
            pl.BlockSpec((HEADS, tm, LANES), lambda i: (0, i, 0)),
            pl.BlockSpec((HEADS, tm, LANES), lambda i: (0, i, 0)),
            pl.BlockSpec((tm, d), lambda i: (i, 0)),
            pl.BlockSpec((width, d), lambda i: (0, 0)),
        ],
        out_specs=pl.BlockSpec((tm, d), lambda i: (i, 0)),
        out_shape=jax.ShapeDtypeStruct((n, d), F32),
        scratch_shapes=[pltpu.VMEM((tm, width), BF16)],
        compiler_params=pltpu.CompilerParams(
            dimension_semantics=("parallel",),
            vmem_limit_bytes=_vmem_limit(blocks, _nbytes((tm, width), BF16))),
        name="outproj",
    )(ohg, oda, x, w)


def _ffn_kernel(x_ref, nw_ref, wg_ref, wu_ref, wd_ref, o_ref, h_s):
    @pl.when(pl.program_id(1) == 0)
    def _():
        x = x_ref[...]
        var = jnp.mean(x * x, axis=-1, keepdims=True)
        h_s[...] = (x * lax.rsqrt(var + EPS) * nw_ref[...]).astype(BF16)
        o_ref[...] = x

    h = h_s[...]
    g = _dot(h, wg_ref[...])
    u = _dot(h, wu_ref[...])
    o_ref[...] += _dot((g * _sigmoid(g) * u).astype(BF16), wd_ref[...])


def _ffn(x, nw, wg, wu, wd, *, tm=512, tf=512):
    n, d = x.shape
    ff = wg.shape[1]
    tm = min(tm, n)
    blocks = (2 * _nbytes((tm, d), F32) + 3 * _nbytes((d, tf), BF16))
    scratch = _nbytes((tm, d), BF16) + 3 * _nbytes((tm, tf), F32)
    return pl.pallas_call(
        _ffn_kernel,
        grid=(n // tm, ff // tf),
        in_specs=[
            pl.BlockSpec((tm, d), lambda i, f: (i, 0)),
            pl.BlockSpec((1, d), lambda i, f: (0, 0)),
            pl.BlockSpec((d, tf), lambda i, f: (0, f)),
            pl.BlockSpec((d, tf), lambda i, f: (0, f)),
            pl.BlockSpec((tf, d), lambda i, f: (f, 0)),
        ],
        out_specs=pl.BlockSpec((tm, d), lambda i, f: (i, 0)),
        out_shape=jax.ShapeDtypeStruct((n, d), F32),
        scratch_shapes=[pltpu.VMEM((tm, d), BF16)],
        compiler_params=pltpu.CompilerParams(
            dimension_semantics=("parallel", "arbitrary"),
            vmem_limit_bytes=_vmem_limit(blocks, scratch)),
        name="ffn",
    )(x, nw.reshape(1, d), wg, wu, wd)


def kernel(x, norm_mix_w, w_in, hg_lb_logits, hg_onorm_w, da_qnorm_w, da_knorm_w, da_lambda,
           da_subln_w, w_out, norm_ffn_w, w_gate, w_up, w_down):
    batch, seq, d = x.shape
    depth = w_in.shape[0]
    hg = d // 2
    assert hg == HEADS * LANES and seq % CHUNK == 0
    xf = x.reshape(batch * seq, d).astype(F32)
    lb_logits = hg_lb_logits.astype(F32).reshape(2 * depth, hg)

    for l in range(depth):
        w_z = w_in[l][:, hg:3 * hg].astype(BF16)
        w_vt = w_in[l][:, 3 * hg:4 * hg].T.astype(BF16)
        w_r = jnp.concatenate([w_in[l][:, :hg], w_in[l][:, 4 * hg:]], axis=1).astype(BF16)
        pz = _proj(xf, norm_mix_w[l], w_z, F32)
        pvt = _proj(xf, norm_mix_w[l], w_vt, BF16, transposed=True)
        pb = _proj(xf, norm_mix_w[l], w_r, BF16)
        ohg = _hgrn(pb, pvt, pz, lb_logits, hg_onorm_w[l], layer=l, batch=batch, seq=seq)
        oda = _attn(pb, da_qnorm_w[l], da_knorm_w[l], da_lambda[l].astype(F32), da_subln_w[l],
                    layer=l, batch=batch, seq=seq)
        x1 = _outproj(ohg, oda, xf, w_out[l].astype(BF16))
        xf = _ffn(x1, norm_ffn_w[l], w_gate[l].astype(BF16), w_up[l].astype(BF16),
                  w_down[l].astype(BF16))
    return xf.reshape(batch, seq, d).astype(x.dtype)
```

```python
import functools
import math

import jax
import jax.numpy as jnp
from jax import lax
from jax.experimental import pallas as pl
from jax.experimental.pallas import tpu as pltpu

F32 = jnp.float32
BF16 = jnp.bfloat16

LANES = 128
HEADS = 8
EPS = 1e-6
LB_FLOOR = 1e-30
LB_CEIL = 1.0 - 1e-6
CHUNK = 64
HGRN_GROUP = 2
EXP_CLAMP = 80.0
VMEM_CAP = 60 * 1024 * 1024
VMEM_SLACK = 12 * 1024 * 1024


def _vmem_limit(pipelined_bytes, scratch_bytes=0):
    return int(min(VMEM_CAP, 2 * pipelined_bytes + scratch_bytes + VMEM_SLACK))


def _nbytes(shape, dtype):
    return math.prod(shape) * jnp.dtype(dtype).itemsize


def _dot(a, b):
    return jnp.dot(a, b, preferred_element_type=F32)


def _dot_nt(a, b):
    return lax.dot_general(a, b, (((1,), (1,)), ((), ())), preferred_element_type=F32)


def _sigmoid(x):
    return 1.0 / (1.0 + jnp.exp(-x))


def _proj_kernel(x_ref, nw_ref, w_ref, wt_ref, ob_ref, oz_ref, ot_ref, h_s, *, nb, nz):
    j = pl.program_id(1)

    @pl.when(j == 0)
    def _():
        x = x_ref[...]
        var = jnp.mean(x * x, axis=-1, keepdims=True)
        h_s[...] = (x * lax.rsqrt(var + EPS) * nw_ref[...]).astype(BF16)

    def slabs(o_ref):
        r = _dot(h_s[...], w_ref[...])
        for c in range(o_ref.shape[0]):
            o_ref[c] = r[:, c * LANES:(c + 1) * LANES].astype(o_ref.dtype)

    @pl.when(j < nb)
    def _():
        slabs(ob_ref)

    @pl.when((j >= nb) & (j < nb + nz))
    def _():
        slabs(oz_ref)

    @pl.when(j >= nb + nz)
    def _():
        r = _dot_nt(wt_ref[...], h_s[...])
        for c in range(ot_ref.shape[0]):
            ot_ref[c] = r[c * LANES:(c + 1) * LANES, :].astype(ot_ref.dtype)


def _proj(x, nw, w_bz, cols_b, w_t, *, tm=1024, tn=512):
    n, d = x.shape
    cols_z = w_bz.shape[1] - cols_b
    cols_t = w_t.shape[0]
    tm = min(tm, n)
    nb, nz, nt = cols_b // tn, cols_z // tn, cols_t // tn
    assert nb * tn == cols_b and nz * tn == cols_z and nt * tn == cols_t
    per = tn // LANES
    blocks = (_nbytes((tm, d), F32) + 2 * _nbytes((d, tn), BF16) + _nbytes((tm, tn), BF16)
              + _nbytes((tm, tn), F32) + _nbytes((tm, tn), BF16))
    return pl.pallas_call(
        functools.partial(_proj_kernel, nb=nb, nz=nz),
        grid=(n // tm, nb + nz + nt),
        in_specs=[
            pl.BlockSpec((tm, d), lambda i, j: (i, 0)),
            pl.BlockSpec((1, d), lambda i, j: (0, 0)),
            pl.BlockSpec((d, tn), lambda i, j: (0, jnp.minimum(j, nb + nz - 1))),
            pl.BlockSpec((tn, d), lambda i, j: (jnp.clip(j - nb - nz, 0, nt - 1), 0)),
        ],
        out_specs=[
            pl.BlockSpec((per, tm, LANES), lambda i, j: (jnp.minimum(j, nb - 1), i, 0)),
            pl.BlockSpec((per, tm, LANES), lambda i, j: (jnp.clip(j - nb, 0, nz - 1), i, 0)),
            pl.BlockSpec((per, LANES, tm), lambda i, j: (jnp.clip(j - nb - nz, 0, nt - 1), 0, i)),
        ],
        out_shape=[
            jax.ShapeDtypeStruct((cols_b // LANES, n, LANES), BF16),
            jax.ShapeDtypeStruct((cols_z // LANES, n, LANES), F32),
            jax.ShapeDtypeStruct((cols_t // LANES, LANES, n), BF16),
        ],
        scratch_shapes=[pltpu.VMEM((tm, d), BF16)],
        compiler_params=pltpu.CompilerParams(
            dimension_semantics=("parallel", "arbitrary"),
            vmem_limit_bytes=_vmem_limit(blocks, _nbytes((tm, d), BF16) + _nbytes((tm, tn), F32))),
        name="proj",
    )(x, nw.reshape(1, d), w_bz, w_t)


def _seg_cumsum(x, seg, rev):
    n = x.shape[0]
    pos = lax.broadcasted_iota(jnp.int32, x.shape, 0) & (seg - 1)
    s = 1
    while s < seg:
        if rev:
            x = x + jnp.where(pos < seg - s, pltpu.roll(x, n - s, 0), 0.0)
        else:
            x = x + jnp.where(pos >= s, pltpu.roll(x, s, 0), 0.0)
        s *= 2
    return x


def _rows_bcast(x, rows, reps):
    return jnp.concatenate([jnp.broadcast_to(x[r:r + 1, :], (reps, x.shape[1])) for r in rows], axis=0)


def _hgrn_kernel(q_ref, vt_ref, g_ref, zf_ref, zb_ref, lbl_ref, ow_ref, o_ref,
                 oacc_s, k_s, a_s, v_s, intra_s, *, layer, depth, tblk):
    seq = q_ref.shape[0]
    nblk = seq // tblk
    nchunk = tblk // CHUNK

    def lower_bound(direction):
        lg = lbl_ref[direction * depth:(direction + 1) * depth, :]
        e = jnp.exp(lg - jnp.max(lg, axis=0, keepdims=True))
        p = e / jnp.sum(e, axis=0, keepdims=True)
        lb = jnp.sum(p[:layer + 1], axis=0, keepdims=True) - p[0:1]
        lb = jnp.clip(lb, 0.0, LB_CEIL)
        return lb, jnp.maximum(lb, LB_FLOOR)

    row_c = lax.broadcasted_iota(jnp.int32, (CHUNK, LANES), 0)
    sr = lax.broadcasted_iota(jnp.int32, (tblk, tblk), 0)
    sc_ = lax.broadcasted_iota(jnp.int32, (tblk, tblk), 1)
    same_chunk = (sr // CHUNK) == (sc_ // CHUNK)
    own_lanes = (lax.broadcasted_iota(jnp.int32, (tblk, nchunk * LANES), 0) // CHUNK
                 == lax.broadcasted_iota(jnp.int32, (tblk, nchunk * LANES), 1) // LANES)

    bounds = (lower_bound(0), lower_bound(1))
    z_refs = (zf_ref, zb_ref)

    def fast_block(direction, slot, r0, st):
        rev = direction == 1
        lb, lbp = bounds[direction]
        causal = same_chunk & ((sc_ >= sr) if rev else (sc_ <= sr))
        q = q_ref[pl.ds(r0, tblk), :].astype(F32)
        vt = vt_ref[:, pl.ds(r0, tblk)]
        z = z_refs[direction][pl.ds(r0, tblk), :]
        sg = _sigmoid(z)
        lf = jnp.log(lbp + (1.0 - lb) * sg)
        k = (1.0 - lb) * (1.0 - sg) - (lbp - lb)
        a = _seg_cumsum(lf, CHUNK, rev)
        edge = (lambda c: c * CHUNK) if rev else (lambda c: c * CHUNK + CHUNK - 1)
        a_edge = _rows_bcast(a, [edge(c) for c in range(nchunk)], CHUNK)
        d = a - _rows_bcast(a, [c * CHUNK + CHUNK // 2 for c in range(nchunk)], CHUNK)
        overflow = jnp.max(jnp.abs(d)) > EXP_CLAMP
        e_mid = jnp.exp(jnp.clip(d, -EXP_CLAMP, EXP_CLAMP))
        qs = (q * e_mid).astype(BF16)
        ks = (k / e_mid).astype(BF16)
        qa = (q * jnp.exp(a)).astype(BF16)
        kb = (k * jnp.exp(a_edge - a)).astype(BF16)
        k_s[slot] = k
        a_s[slot] = a

        scores = jnp.where(causal, _dot_nt(qs, ks), 0.0).astype(BF16)
        intra_s[slot] = _dot_nt(scores, vt)
        upd = _dot(vt, jnp.where(own_lanes, jnp.concatenate([kb] * nchunk, axis=1), 0.0))
        inter = [None] * nchunk
        for c in (range(nchunk - 1, -1, -1) if rev else range(nchunk)):
            lo = c * CHUNK
            inter[c] = _dot_nt(qa[lo:lo + CHUNK], st.astype(BF16))
            st = st * jnp.exp(a[edge(c):edge(c) + 1, :]) + upd[:, c * LANES:(c + 1) * LANES]
        return jnp.concatenate(inter, axis=0), overflow, st

    def exact_intra(direction, slot, r0):
        rev = direction == 1
        v_s[...] = vt_ref[:, pl.ds(r0, tblk)].astype(F32).T
        for c in range(nchunk):
            lo = c * CHUNK
            q_c = q_ref[pl.ds(r0 + lo, CHUNK), :].astype(F32)
            a_c = a_s[slot, lo:lo + CHUNK, :]

            def pair(s, acc):
                w = q_c * k_s[slot, pl.ds(lo + s, 1), :] * jnp.exp(
                    jnp.minimum(a_c - a_s[slot, pl.ds(lo + s, 1), :], 0.0))
                r = jnp.sum(w, axis=1, keepdims=True)
                m = (row_c <= s) if rev else (row_c >= s)
                return acc + jnp.where(m, r, 0.0) * v_s[pl.ds(lo + s, 1), :]

            intra_s[slot, lo:lo + CHUNK, :] = lax.fori_loop(
                0, CHUNK, pair, jnp.zeros((CHUNK, LANES), F32))

    def emit(r0, o):
        var = jnp.mean(o * o, axis=-1, keepdims=True)
        gate = g_ref[pl.ds(r0, tblk), :].astype(F32)
        o = o * lax.rsqrt(var + EPS) * ow_ref[...] * (gate * _sigmoid(gate))
        o_ref[pl.ds(r0, tblk), :] = o.astype(o_ref.dtype)

    def sweep(second_half):
        def body(i, carry):
            st = list(carry)
            work = []
            for u in range(HGRN_GROUP):
                blk = i * HGRN_GROUP + u
                for direction in (0, 1):
                    r0 = pl.multiple_of((blk if direction == 0 else nblk - 1 - blk) * tblk, tblk)
                    slot = 2 * u + direction
                    inter, overflow, st[direction] = fast_block(direction, slot, r0, st[direction])
                    work.append((direction, slot, r0, inter, overflow))

            for direction, slot, r0, _, overflow in work:
                @pl.when(overflow)
                def _(direction=direction, slot=slot, r0=r0):
                    exact_intra(direction, slot, r0)

            for _, slot, r0, inter, _ in work:
                o = inter + intra_s[slot]
                if second_half:
                    emit(r0, o + oacc_s[pl.ds(r0, tblk), :])
                else:
                    oacc_s[pl.ds(r0, tblk), :] = o
            return tuple(st)
        return body

    zero = jnp.zeros((LANES, LANES), F32)
    steps = nblk // HGRN_GROUP
    carry = lax.fori_loop(0, steps // 2, sweep(False), (zero, zero))
    lax.fori_loop(steps // 2, steps, sweep(True), carry)


def _hgrn(pb, pvt, pz, lb_logits, onorm_w, *, layer, batch, seq, tblk=256):
    depth = lb_logits.shape[0] // 2
    n = batch * seq
    tblk = min(tblk, seq)

    def slab(base):
        return pl.BlockSpec((None, seq, LANES), lambda b, h: (base + h, b, 0))

    assert seq % (2 * HGRN_GROUP * tblk) == 0
    blocks = 4 * _nbytes((seq, LANES), BF16) + 2 * _nbytes((seq, LANES), F32)
    scratch = _nbytes((seq, LANES), F32) + (1 + 6 * HGRN_GROUP) * _nbytes((tblk, LANES), F32)
    return pl.pallas_call(
        functools.partial(_hgrn_kernel, layer=layer, depth=depth, tblk=tblk),
        grid=(batch, HEADS),
        in_specs=[
            slab(0),
            pl.BlockSpec((None, LANES, seq), lambda b, h: (h, 0, b)),
            slab(HEADS),
            pl.BlockSpec((None, seq, LANES), lambda b, h: (h, b, 0)),
            pl.BlockSpec((None, seq, LANES), lambda b, h: (HEADS + h, b, 0)),
            pl.BlockSpec((2 * depth, LANES), lambda b, h: (0, h)),
            pl.BlockSpec((1, LANES), lambda b, h: (0, 0)),
        ],
        out_specs=pl.BlockSpec((None, seq, LANES), lambda b, h: (h, b, 0)),
        out_shape=jax.ShapeDtypeStruct((HEADS, n, LANES), BF16),
        scratch_shapes=[
            pltpu.VMEM((seq, LANES), F32),
            pltpu.VMEM((2 * HGRN_GROUP, tblk, LANES), F32),
            pltpu.VMEM((2 * HGRN_GROUP, tblk, LANES), F32),
            pltpu.VMEM((tblk, LANES), F32),
            pltpu.VMEM((2 * HGRN_GROUP, tblk, LANES), F32),
        ],
        compiler_params=pltpu.CompilerParams(
            dimension_semantics=("parallel", "parallel"),
            vmem_limit_bytes=_vmem_limit(blocks, scratch)),
        name="hgrn",
    )(pb, pvt, pb, pz, pz, lb_logits, onorm_w.reshape(1, LANES))


HALF = LANES // 2
LANE_SHIFT = HALF
LANE_THI = HALF + 1
LANE_TLO = HALF + 2
LANE_SHI = HALF + 3
LANE_SLO = HALF + 4
POS_SPLIT = 256
ROWSUM_MIN = 1e-24


def _half_rms_scale(x, lo_mask):
    sq = x * x
    half = x.shape[1] // 2
    ss_lo = jnp.sum(jnp.where(lo_mask, sq, 0.0), axis=1, keepdims=True)
    ss_hi = jnp.sum(jnp.where(lo_mask, 0.0, sq), axis=1, keepdims=True)
    return jnp.where(lo_mask, lax.rsqrt(ss_lo / half + EPS), lax.rsqrt(ss_hi / half + EPS))


def _split_pos(pos):
    return (pos & ~(POS_SPLIT - 1)).astype(F32), (pos & (POS_SPLIT - 1)).astype(F32)


def _attn_kernel(q_ref, k_ref, v_ref, qw_ref, kw_ref, lam_ref, sw_ref, o_ref,
                 k1_s, k2_s, va_s, kpos_s, qpos_s, dist_s, q_s, bias_s, acc1_s, acc2_s, m_s, l_s, accx_s,
                 *, tk, lam_init):
    tq = q_ref.shape[0]
    seq = k_ref.shape[0]
    nk = seq // tk
    nvar = tk // tq
    h = pl.program_id(1)
    qi = pl.program_id(2)
    lane = lax.broadcasted_iota(jnp.int32, (1, LANES), 1)
    lo_mask = lane < HALF
    slope = lax.bitcast_convert_type(jnp.full((1, LANES), (126 - h) << 23, jnp.int32), F32)

    @pl.when((pl.program_id(0) == 0) & (h == 0) & (qi == 0))
    def _():
        hi, lo = _split_pos(lax.broadcasted_iota(jnp.int32, (seq, LANES), 0))
        kpos_s[...] = jnp.where(lane == LANE_SHI, hi, jnp.where(
            lane == LANE_SLO, lo, jnp.where((lane >= LANE_SHIFT) & (lane <= LANE_TLO), 1.0, 0.0)))
        qpos_s[...] = jnp.where(lane == LANE_THI, -hi, jnp.where(
            lane == LANE_TLO, -lo, jnp.where((lane == LANE_SHI) | (lane == LANE_SLO), 1.0, 0.0)))
        rel = (lax.broadcasted_iota(jnp.int32, (tq, tk), 0) - lax.broadcasted_iota(jnp.int32, (tq, tk), 1))
        for v in range(nvar):
            dist_s[v] = jnp.abs(rel + v * tq).astype(F32)
        bias_s[0] = jnp.zeros((tq, tk), F32)

    @pl.when(qi == 0)
    def _():
        for c in range(nk):
            rows = slice(c * tk, (c + 1) * tk)
            kf = k_ref[rows, :].astype(F32)
            kn = kf * _half_rms_scale(kf, lo_mask) * kw_ref[...]
            pos = kpos_s[rows, :]
            k1_s[rows, :] = jnp.where(lo_mask, kn, pos).astype(BF16)
            k2_s[rows, :] = jnp.where(lo_mask, pltpu.roll(kn, HALF, 1), pos).astype(BF16)
            va_s[rows, :LANES] = v_ref[rows, :]
            va_s[rows, LANES:] = jnp.ones((tk, LANES), BF16)
        for v in range(nvar):
            bias_s[1 + v] = dist_s[v] * (-slope[:, :1])

    def gain_max(w_ref, mask):
        return jnp.max(jnp.where(mask, jnp.abs(w_ref[...]), 0.0), axis=1, keepdims=True)

    m1 = math.sqrt(HALF) * gain_max(qw_ref, lo_mask) * gain_max(kw_ref, lo_mask)
    m2 = math.sqrt(HALF) * gain_max(qw_ref, ~lo_mask) * gain_max(kw_ref, ~lo_mask)

    qf = q_ref[...].astype(F32)
    qn = qf * _half_rms_scale(qf, lo_mask) * qw_ref[...] * (HALF ** -0.5)
    q1 = jnp.where(lo_mask, qn, 0.0)
    q2 = jnp.where(lo_mask, pltpu.roll(qn, HALF, 1), 0.0)
    q1m = jnp.where(lane == LANE_SHIFT, -m1, q1)
    q2m = jnp.where(lane == LANE_SHIFT, -m2, q2)
    pen = qpos_s[pl.ds(pl.multiple_of(qi * tq, tq), tq), :] * slope
    for v, sigma in enumerate((1.0, -1.0, 0.0)):
        q_s[v] = (q1m + sigma * pen).astype(BF16)
        q_s[3 + v] = (q2m + sigma * pen).astype(BF16)

    kjm = (qi * tq) // tk
    diag_tile = 1 + (qi * tq - kjm * tk) // tq

    for kj in range(nk):
        rows = slice(kj * tk, (kj + 1) * tk)
        diag = kjm == kj
        ver = jnp.where(diag, 2, jnp.where(kjm > kj, 0, 1))
        bias = bias_s[jnp.where(diag, diag_tile, 0)]
        vv = va_s[rows, :]
        p1 = jnp.exp(_dot_nt(q_s[ver], k1_s[rows, :]) + bias)
        p2 = jnp.exp(_dot_nt(q_s[3 + ver], k2_s[rows, :]) + bias)
        if kj == 0:
            acc1_s[...] = _dot(p1.astype(BF16), vv)
            acc2_s[...] = _dot(p2.astype(BF16), vv)
        else:
            acc1_s[...] += _dot(p1.astype(BF16), vv)
            acc2_s[...] += _dot(p2.astype(BF16), vv)

    lam_p = lam_ref[...]
    lam = (jnp.exp(jnp.sum(lam_p[0:1] * lam_p[1:2], axis=1, keepdims=True))
           - jnp.exp(jnp.sum(lam_p[2:3] * lam_p[3:4], axis=1, keepdims=True)) + lam_init)

    def finish(o1, o2):
        o = o1 - lam * o2
        var = jnp.mean(o * o, axis=-1, keepdims=True)
        o = o * lax.rsqrt(var + EPS) * sw_ref[...] * (1.0 - lam_init)
        o_ref[...] = o.astype(o_ref.dtype)

    a1 = acc1_s[...]
    a2 = acc2_s[...]
    l1 = a1[:, LANES:]
    l2 = a2[:, LANES:]
    finish(a1[:, :LANES] / l1, a2[:, :LANES] / l2)

    @pl.when(jnp.minimum(jnp.min(l1), jnp.min(l2)) < ROWSUM_MIN)
    def _():
        q_both = jnp.concatenate([q1, q2], axis=0).astype(BF16)
        m_s[...] = jnp.full(m_s.shape, -jnp.inf, F32)
        l_s[...] = jnp.zeros(l_s.shape, F32)
        accx_s[...] = jnp.zeros(accx_s.shape, F32)
        rel = (lax.broadcasted_iota(jnp.int32, (tq, tk), 0) - lax.broadcasted_iota(jnp.int32, (tq, tk), 1)
               + qi * tq).astype(F32)

        def exact_body(kj, carry):
            k0 = pl.multiple_of(kj * tk, tk)
            s = jnp.concatenate([_dot_nt(q_both[:tq], k1_s[pl.ds(k0, tk), :]),
                                 _dot_nt(q_both[tq:], k2_s[pl.ds(k0, tk), :])], axis=0)
            pen = jnp.abs(rel - lax.convert_element_type(kj * tk, F32)) * (-slope[:, :1])
            s = s + jnp.concatenate([pen, pen], axis=0)
            m_prev = m_s[...]
            m_new = jnp.maximum(m_prev, jnp.max(s, axis=1, keepdims=True))
            alpha = jnp.exp(m_prev - m_new)
            p = jnp.exp(s - m_new)
            l_s[...] = alpha * l_s[...] + jnp.sum(p, axis=1, keepdims=True)
            accx_s[...] = alpha * accx_s[...] + _dot(p.astype(BF16), va_s[pl.ds(k0, tk), :LANES])
            m_s[...] = m_new
            return carry

        lax.fori_loop(0, nk, exact_body, 0)
        acc = accx_s[...]
        l = l_s[...]
        finish(acc[:tq] / l[:tq], acc[tq:] / l[tq:])


def _attn(pb, qn_w, kn_w, lam_p, subln_w, *, layer, batch, seq, tq=512, tk=512):
    n = batch * seq
    tq = min(tq, seq)
    tk = min(tk, seq)
    nq = seq // tq
    assert tk % tq == 0 and seq % tk == 0 and seq <= POS_SPLIT * POS_SPLIT
    lam_init = 0.8 - 0.6 * math.exp(-0.3 * layer)
    blocks = 2 * _nbytes((tq, LANES), BF16) + 2 * _nbytes((seq, LANES), BF16)
    scratch = (4 * _nbytes((seq, LANES), BF16) + 2 * _nbytes((seq, LANES), F32)
               + 6 * _nbytes((tq, LANES), BF16)
               + (1 + 2 * (tk // tq)) * _nbytes((tq, tk), F32) + 2 * _nbytes((tq, 2 * LANES), F32)
               + 3 * _nbytes((2 * tq, LANES), F32) + 4 * _nbytes((2 * tq, tk), F32))
    return pl.pallas_call(
        functools.partial(_attn_kernel, tk=tk, lam_init=lam_init),
        grid=(batch, HEADS, nq),
        in_specs=[
            pl.BlockSpec((None, tq, LANES), lambda b, h, i: (2 * HEADS + h, b * nq + i, 0)),
            pl.BlockSpec((None, seq, LANES), lambda b, h, i: (3 * HEADS + h, b, 0)),
            pl.BlockSpec((None, seq, LANES), lambda b, h, i: (4 * HEADS + h, b, 0)),
            pl.BlockSpec((1, LANES), lambda b, h, i: (0, 0)),
            pl.BlockSpec((1, LANES), lambda b, h, i: (0, 0)),
            pl.BlockSpec((4, LANES // 2), lambda b, h, i: (0, 0)),
            pl.BlockSpec((1, LANES), lambda b, h, i: (0, 0)),
        ],
        out_specs=pl.BlockSpec((None, tq, LANES), lambda b, h, i: (h, b * nq + i, 0)),
        out_shape=jax.ShapeDtypeStruct((HEADS, n, LANES), BF16),
        scratch_shapes=[
            pltpu.VMEM((seq, LANES), BF16),
            pltpu.VMEM((seq, LANES), BF16),
            pltpu.VMEM((seq, 2 * LANES), BF16),
            pltpu.VMEM((seq, LANES), F32),
            pltpu.VMEM((seq, LANES), F32),
            pltpu.VMEM((tk // tq, tq, tk), F32),
            pltpu.VMEM((6, tq, LANES), BF16),
            pltpu.VMEM((1 + tk // tq, tq, tk), F32),
            pltpu.VMEM((tq, 2 * LANES), F32),
            pltpu.VMEM((tq, 2 * LANES), F32),
            pltpu.VMEM((2 * tq, 1), F32),
            pltpu.VMEM((2 * tq, 1), F32),
            pltpu.VMEM((2 * tq, LANES), F32),
        ],
        compiler_params=pltpu.CompilerParams(
            dimension_semantics=("arbitrary", "arbitrary", "arbitrary"),
            vmem_limit_bytes=_vmem_limit(blocks, scratch)),
        name="diffattn",
    )(pb, pb, pb, jnp.tile(qn_w, 2).reshape(1, LANES), jnp.tile(kn_w, 2).reshape(1, LANES),
      lam_p, subln_w.reshape(1, LANES))


def _outproj_kernel(ohg_ref, oda_ref, x_ref, w_ref, o_ref, mix_s):
    for c in range(HEADS):
        mix_s[:, c * LANES:(c + 1) * LANES] = ohg_ref[c]
        mix_s[:, (HEADS + c) * LANES:(HEADS + c + 1) * LANES] = oda_ref[c]
    o_ref[...] = x_ref[...] + _dot(mix_s[...], w_ref[...])


def _outproj(ohg, oda, x, w, *, tm=512):
    n, d = x.shape
    width = w.shape[0]
    tm = min(tm, n)
    blocks = (2 * _nbytes((HEADS, tm, LANES), BF16) + 2 * _nbytes((tm, d), F32) + _nbytes((width, d), BF16))
    return pl.pallas_call(
        _outproj_kernel,
        grid=(n // tm,),
        in_specs=[
            pl.BlockSpec((HEADS, tm, LANES), lambda i: (0, i, 0)),
            pl.BlockSpec((HEADS, tm, LANES), lambda i: (0, i, 0)),
            pl.BlockSpec((tm, d), lambda i: (i, 0)),
            pl.BlockSpec((width, d), lambda i: (0, 0)),
        ],
        out_specs=pl.BlockSpec((tm, d), lambda i: (i, 0)),
        out_shape=jax.ShapeDtypeStruct((n, d), F32),
        scratch_shapes=[pltpu.VMEM((tm, width), BF16)],
        compiler_params=pltpu.CompilerParams(
            dimension_semantics=("parallel",),
            vmem_limit_bytes=_vmem_limit(blocks, _nbytes((tm, width), BF16))),
        name="outproj",
    )(ohg, oda, x, w)


def _ffn_kernel(x_ref, nw_ref, wg_ref, wu_ref, wd_ref, o_ref, h_s):
    @pl.when(pl.program_id(1) == 0)
    def _():
        x = x_ref[...]
        var = jnp.mean(x * x, axis=-1, keepdims=True)
        h_s[...] = (x * lax.rsqrt(var + EPS) * nw_ref[...]).astype(BF16)
        o_ref[...] = x

    h = h_s[...]
    g = _dot(h, wg_ref[...])
    u = _dot(h, wu_ref[...])
    o_ref[...] += _dot((g * _sigmoid(g) * u).astype(BF16), wd_ref[...])


def _ffn(x, nw, wg, wu, wd, *, tm=1024, tf=512):
    n, d = x.shape
    ff = wg.shape[1]
    tm = min(tm, n)
    blocks = (2 * _nbytes((tm, d), F32) + 3 * _nbytes((d, tf), BF16))
    scratch = _nbytes((tm, d), BF16) + 3 * _nbytes((tm, tf), F32)
    return pl.pallas_call(
        _ffn_kernel,
        grid=(n // tm, ff // tf),
        in_specs=[
            pl.BlockSpec((tm, d), lambda i, f: (i, 0)),
            pl.BlockSpec((1, d), lambda i, f: (0, 0)),
            pl.BlockSpec((d, tf), lambda i, f: (0, f)),
            pl.BlockSpec((d, tf), lambda i, f: (0, f)),
            pl.BlockSpec((tf, d), lambda i, f: (f, 0)),
        ],
        out_specs=pl.BlockSpec((tm, d), lambda i, f: (i, 0)),
        out_shape=jax.ShapeDtypeStruct((n, d), F32),
        scratch_shapes=[pltpu.VMEM((tm, d), BF16)],
        compiler_params=pltpu.CompilerParams(
            dimension_semantics=("parallel", "arbitrary"),
            vmem_limit_bytes=_vmem_limit(blocks, scratch)),
        name="ffn",
    )(x, nw.reshape(1, d), wg, wu, wd)


def kernel(x, norm_mix_w, w_in, hg_lb_logits, hg_onorm_w, da_qnorm_w, da_knorm_w, da_lambda,
           da_subln_w, w_out, norm_ffn_w, w_gate, w_up, w_down):
    batch, seq, d = x.shape
    depth = w_in.shape[0]
    hg = d // 2
    assert hg == HEADS * LANES and seq % CHUNK == 0
    xf = x.reshape(batch * seq, d).astype(F32)
    lb_logits = hg_lb_logits.astype(F32).reshape(2 * depth, hg)

    for l in range(depth):
        w_bz = jnp.concatenate([w_in[l][:, :hg], w_in[l][:, 4 * hg:], w_in[l][:, hg:3 * hg]],
                               axis=1).astype(BF16)
        w_vt = w_in[l][:, 3 * hg:4 * hg].T.astype(BF16)
        pb, pz, pvt = _proj(xf, norm_mix_w[l], w_bz, 5 * hg, w_vt)
        ohg = _hgrn(pb, pvt, pz, lb_logits, hg_onorm_w[l], layer=l, batch=batch, seq=seq)
        oda = _attn(pb, da_qnorm_w[l], da_knorm_w[l], da_lambda[l].astype(F32), da_subln_w[l],
                    layer=l, batch=batch, seq=seq)
        x1 = _outproj(ohg, oda, xf, w_out[l].astype(BF16))
        xf = _ffn(x1, norm_ffn_w[l], w_gate[l].astype(BF16), w_up[l].astype(BF16),
                  w_down[l].astype(BF16))
    return xf.reshape(batch, seq, d).astype(x.dtype)
```

```python
import functools
import math

import jax
import jax.numpy as jnp
from jax import lax
from jax.experimental import pallas as pl
from jax.experimental.pallas import tpu as pltpu

F32 = jnp.float32
BF16 = jnp.bfloat16

LANES = 128
HEADS = 8
EPS = 1e-6
LB_FLOOR = 1e-30
LB_CEIL = 1.0 - 1e-6
CHUNK = 64
HGRN_GROUP = 2
EXP_CLAMP = 80.0
VMEM_CAP = 60 * 1024 * 1024
VMEM_SLACK = 12 * 1024 * 1024


def _vmem_limit(pipelined_bytes, scratch_bytes=0):
    return int(min(VMEM_CAP, 2 * pipelined_bytes + scratch_bytes + VMEM_SLACK))


def _nbytes(shape, dtype):
    return math.prod(shape) * jnp.dtype(dtype).itemsize


def _dot(a, b):
    return jnp.dot(a, b, preferred_element_type=F32)


def _dot_nt(a, b):
    return lax.dot_general(a, b, (((1,), (1,)), ((), ())), preferred_element_type=F32)


def _sigmoid(x):
    return 1.0 / (1.0 + jnp.exp(-x))


def _proj_kernel(x_ref, nw_ref, w_ref, wt_ref, ob_ref, oz_ref, ot_ref, h_s, *, nb, nz):
    j = pl.program_id(1)

    @pl.when(j == 0)
    def _():
        x = x_ref[...]
        var = jnp.mean(x * x, axis=-1, keepdims=True)
        h_s[...] = (x * lax.rsqrt(var + EPS) * nw_ref[...]).astype(BF16)

    def slabs(o_ref):
        r = _dot(h_s[...], w_ref[...])
        for c in range(o_ref.shape[0]):
            o_ref[c] = r[:, c * LANES:(c + 1) * LANES].astype(o_ref.dtype)

    @pl.when(j < nb)
    def _():
        slabs(ob_ref)

    @pl.when((j >= nb) & (j < nb + nz))
    def _():
        slabs(oz_ref)

    @pl.when(j >= nb + nz)
    def _():
        r = _dot_nt(wt_ref[...], h_s[...])
        for c in range(ot_ref.shape[0]):
            ot_ref[c] = r[c * LANES:(c + 1) * LANES, :].astype(ot_ref.dtype)


def _pick(j, values):
    out = values[-1]
    for idx in range(len(values) - 2, -1, -1):
        out = jnp.where(j <= idx, values[idx], out)
    return out


def _wt_kernel(w_ref, o_ref):
    o_ref[...] = w_ref[...].T.astype(o_ref.dtype)


def _transposed_columns(w, col_tiles, *, t=512):
    depth, d, _ = w.shape
    return pl.pallas_call(
        _wt_kernel,
        grid=(depth, len(col_tiles), d // t),
        in_specs=[pl.BlockSpec((None, t, t), lambda l, c, r: (l, r, _pick(c, col_tiles)))],
        out_specs=pl.BlockSpec((None, t, t), lambda l, c, r: (l, c, r)),
        out_shape=jax.ShapeDtypeStruct((depth, len(col_tiles) * t, d), BF16),
        compiler_params=pltpu.CompilerParams(
            dimension_semantics=("parallel", "parallel", "parallel"),
            vmem_limit_bytes=_vmem_limit(_nbytes((t, t), F32) + _nbytes((t, t), BF16), 2 * _nbytes((t, t), F32))),
        name="wtranspose",
    )(w)


def _proj(x, nw, w, wt, layer, tiles_b, tiles_z, *, tm=1024, tn=1024):
    n, d = x.shape
    tm = min(tm, n)
    nb, nz, nt = len(tiles_b), len(tiles_z), wt.shape[1] // tn
    assert nt * tn == wt.shape[1] and w.shape[2] % tn == 0
    per = tn // LANES
    main_tiles = tuple(tiles_b) + tuple(tiles_z)
    blocks = (_nbytes((tm, d), F32) + 2 * _nbytes((d, tn), BF16) + _nbytes((tm, tn), BF16)
              + _nbytes((tm, tn), F32) + _nbytes((tm, tn), BF16))
    return pl.pallas_call(
        functools.partial(_proj_kernel, nb=nb, nz=nz),
        grid=(n // tm, nb + nz + nt),
        in_specs=[
            pl.BlockSpec((tm, d), lambda i, j: (i, 0)),
            pl.BlockSpec((1, d), lambda i, j: (0, 0)),
            pl.BlockSpec((None, d, tn), lambda i, j: (layer, 0, _pick(j, main_tiles))),
            pl.BlockSpec((None, tn, d), lambda i, j: (layer, jnp.clip(j - nb - nz, 0, nt - 1), 0)),
        ],
        out_specs=[
            pl.BlockSpec((per, tm, LANES), lambda i, j: (jnp.minimum(j, nb - 1), i, 0)),
            pl.BlockSpec((per, tm, LANES), lambda i, j: (jnp.clip(j - nb, 0, nz - 1), i, 0)),
            pl.BlockSpec((per, LANES, tm), lambda i, j: (jnp.clip(j - nb - nz, 0, nt - 1), 0, i)),
        ],
        out_shape=[
            jax.ShapeDtypeStruct((nb * per, n, LANES), BF16),
            jax.ShapeDtypeStruct((nz * per, n, LANES), F32),
            jax.ShapeDtypeStruct((nt * per, LANES, n), BF16),
        ],
        scratch_shapes=[pltpu.VMEM((tm, d), BF16)],
        compiler_params=pltpu.CompilerParams(
            dimension_semantics=("parallel", "arbitrary"),
            vmem_limit_bytes=_vmem_limit(blocks, _nbytes((tm, d), BF16) + _nbytes((tm, tn), F32))),
        name="proj",
    )(x, nw.reshape(1, d), w, wt)


def _seg_cumsum(x, seg, rev):
    n = x.shape[0]
    pos = lax.broadcasted_iota(jnp.int32, x.shape, 0) & (seg - 1)
    s = 1
    while s < seg:
        if rev:
            x = x + jnp.where(pos < seg - s, pltpu.roll(x, n - s, 0), 0.0)
        else:
            x = x + jnp.where(pos >= s, pltpu.roll(x, s, 0), 0.0)
        s *= 2
    return x


def _rows_bcast(x, rows, reps):
    return jnp.concatenate([jnp.broadcast_to(x[r:r + 1, :], (reps, x.shape[1])) for r in rows], axis=0)


def _hgrn_kernel(q_ref, vt_ref, g_ref, zf_ref, zb_ref, lbl_ref, ow_ref, o_ref,
                 oacc_s, k_s, a_s, v_s, intra_s, *, layer, depth, tblk):
    seq = q_ref.shape[0]
    nblk = seq // tblk
    nchunk = tblk // CHUNK

    def lower_bound(direction):
        lg = lbl_ref[direction * depth:(direction + 1) * depth, :]
        e = jnp.exp(lg - jnp.max(lg, axis=0, keepdims=True))
        p = e / jnp.sum(e, axis=0, keepdims=True)
        lb = jnp.sum(p[:layer + 1], axis=0, keepdims=True) - p[0:1]
        lb = jnp.clip(lb, 0.0, LB_CEIL)
        return lb, jnp.maximum(lb, LB_FLOOR)

    row_c = lax.broadcasted_iota(jnp.int32, (CHUNK, LANES), 0)
    sr = lax.broadcasted_iota(jnp.int32, (tblk, tblk), 0)
    sc_ = lax.broadcasted_iota(jnp.int32, (tblk, tblk), 1)
    same_chunk = (sr // CHUNK) == (sc_ // CHUNK)
    own_lanes = (lax.broadcasted_iota(jnp.int32, (tblk, nchunk * LANES), 0) // CHUNK
                 == lax.broadcasted_iota(jnp.int32, (tblk, nchunk * LANES), 1) // LANES)

    bounds = (lower_bound(0), lower_bound(1))
    z_refs = (zf_ref, zb_ref)

    def fast_block(direction, slot, r0, st):
        rev = direction == 1
        lb, lbp = bounds[direction]
        causal = same_chunk & ((sc_ >= sr) if rev else (sc_ <= sr))
        q = q_ref[pl.ds(r0, tblk), :].astype(F32)
        vt = vt_ref[:, pl.ds(r0, tblk)]
        z = z_refs[direction][pl.ds(r0, tblk), :]
        sg = _sigmoid(z)
        lf = jnp.log(lbp + (1.0 - lb) * sg)
        k = (1.0 - lb) * (1.0 - sg) - (lbp - lb)
        a = _seg_cumsum(lf, CHUNK, rev)
        edge = (lambda c: c * CHUNK) if rev else (lambda c: c * CHUNK + CHUNK - 1)
        a_edge = _rows_bcast(a, [edge(c) for c in range(nchunk)], CHUNK)
        d = a - _rows_bcast(a, [c * CHUNK + CHUNK // 2 for c in range(nchunk)], CHUNK)
        overflow = jnp.max(jnp.abs(d)) > EXP_CLAMP
        e_mid = jnp.exp(jnp.clip(d, -EXP_CLAMP, EXP_CLAMP))
        qs = (q * e_mid).astype(BF16)
        ks = (k / e_mid).astype(BF16)
        qa = (q * jnp.exp(a)).astype(BF16)
        kb = (k * jnp.exp(a_edge - a)).astype(BF16)
        k_s[slot] = k
        a_s[slot] = a

        scores = jnp.where(causal, _dot_nt(qs, ks), 0.0).astype(BF16)
        intra_s[slot] = _dot_nt(scores, vt)
        upd = _dot(vt, jnp.where(own_lanes, jnp.concatenate([kb] * nchunk, axis=1), 0.0))
        inter = [None] * nchunk
        for c in (range(nchunk - 1, -1, -1) if rev else range(nchunk)):
            lo = c * CHUNK
            inter[c] = _dot_nt(qa[lo:lo + CHUNK], st.astype(BF16))
            st = st * jnp.exp(a[edge(c):edge(c) + 1, :]) + upd[:, c * LANES:(c + 1) * LANES]
        return jnp.concatenate(inter, axis=0), overflow, st

    def exact_intra(direction, slot, r0):
        rev = direction == 1
        v_s[...] = vt_ref[:, pl.ds(r0, tblk)].astype(F32).T
        for c in range(nchunk):
            lo = c * CHUNK
            q_c = q_ref[pl.ds(r0 + lo, CHUNK), :].astype(F32)
            a_c = a_s[slot, lo:lo + CHUNK, :]

            def pair(s, acc):
                w = q_c * k_s[slot, pl.ds(lo + s, 1), :] * jnp.exp(
                    jnp.minimum(a_c - a_s[slot, pl.ds(lo + s, 1), :], 0.0))
                r = jnp.sum(w, axis=1, keepdims=True)
                m = (row_c <= s) if rev else (row_c >= s)
                return acc + jnp.where(m, r, 0.0) * v_s[pl.ds(lo + s, 1), :]

            intra_s[slot, lo:lo + CHUNK, :] = lax.fori_loop(
                0, CHUNK, pair, jnp.zeros((CHUNK, LANES), F32))

    def emit(r0, o):
        var = jnp.mean(o * o, axis=-1, keepdims=True)
        gate = g_ref[pl.ds(r0, tblk), :].astype(F32)
        o = o * lax.rsqrt(var + EPS) * ow_ref[...] * (gate * _sigmoid(gate))
        o_ref[pl.ds(r0, tblk), :] = o.astype(o_ref.dtype)

    def sweep(second_half):
        def body(i, carry):
            st = list(carry)
            work = []
            for u in range(HGRN_GROUP):
                blk = i * HGRN_GROUP + u
                for direction in (0, 1):
                    r0 = pl.multiple_of((blk if direction == 0 else nblk - 1 - blk) * tblk, tblk)
                    slot = 2 * u + direction
                    inter, overflow, st[direction] = fast_block(direction, slot, r0, st[direction])
                    work.append((direction, slot, r0, inter, overflow))

            for direction, slot, r0, _, overflow in work:
                @pl.when(overflow)
                def _(direction=direction, slot=slot, r0=r0):
                    exact_intra(direction, slot, r0)

            for _, slot, r0, inter, _ in work:
                o = inter + intra_s[slot]
                if second_half:
                    emit(r0, o + oacc_s[pl.ds(r0, tblk), :])
                else:
                    oacc_s[pl.ds(r0, tblk), :] = o
            return tuple(st)
        return body

    zero = jnp.zeros((LANES, LANES), F32)
    steps = nblk // HGRN_GROUP
    carry = lax.fori_loop(0, steps // 2, sweep(False), (zero, zero))
    lax.fori_loop(steps // 2, steps, sweep(True), carry)


def _hgrn(pb, pvt, pz, lb_logits, onorm_w, *, layer, batch, seq, tblk=256):
    depth = lb_logits.shape[0] // 2
    n = batch * seq
    tblk = min(tblk, seq)

    def slab(base):
        return pl.BlockSpec((None, seq, LANES), lambda b, h: (base + h, b, 0))

    assert seq % (2 * HGRN_GROUP * tblk) == 0
    blocks = 4 * _nbytes((seq, LANES), BF16) + 2 * _nbytes((seq, LANES), F32)
    scratch = _nbytes((seq, LANES), F32) + (1 + 6 * HGRN_GROUP) * _nbytes((tblk, LANES), F32)
    return pl.pallas_call(
        functools.partial(_hgrn_kernel, layer=layer, depth=depth, tblk=tblk),
        grid=(batch, HEADS),
        in_specs=[
            slab(0),
            pl.BlockSpec((None, LANES, seq), lambda b, h: (h, 0, b)),
            slab(HEADS),
            pl.BlockSpec((None, seq, LANES), lambda b, h: (h, b, 0)),
            pl.BlockSpec((None, seq, LANES), lambda b, h: (HEADS + h, b, 0)),
            pl.BlockSpec((2 * depth, LANES), lambda b, h: (0, h)),
            pl.BlockSpec((1, LANES), lambda b, h: (0, 0)),
        ],
        out_specs=pl.BlockSpec((None, seq, LANES), lambda b, h: (h, b, 0)),
        out_shape=jax.ShapeDtypeStruct((HEADS, n, LANES), BF16),
        scratch_shapes=[
            pltpu.VMEM((seq, LANES), F32),
            pltpu.VMEM((2 * HGRN_GROUP, tblk, LANES), F32),
            pltpu.VMEM((2 * HGRN_GROUP, tblk, LANES), F32),
            pltpu.VMEM((tblk, LANES), F32),
            pltpu.VMEM((2 * HGRN_GROUP, tblk, LANES), F32),
        ],
        compiler_params=pltpu.CompilerParams(
            dimension_semantics=("parallel", "parallel"),
            vmem_limit_bytes=_vmem_limit(blocks, scratch)),
        name="hgrn",
    )(pb, pvt, pb, pz, pz, lb_logits, onorm_w.reshape(1, LANES))


HALF = LANES // 2
LANE_SHIFT = HALF
LANE_THI = HALF + 1
LANE_TLO = HALF + 2
LANE_SHI = HALF + 3
LANE_SLO = HALF + 4
POS_SPLIT = 256
ROWSUM_MIN = 1e-24
ONES_ROWS = 16


def _half_rms_scale(x, lo_mask):
    sq = x * x
    half = x.shape[1] // 2
    ss_lo = jnp.sum(jnp.where(lo_mask, sq, 0.0), axis=1, keepdims=True)
    ss_hi = jnp.sum(jnp.where(lo_mask, 0.0, sq), axis=1, keepdims=True)
    return jnp.where(lo_mask, lax.rsqrt(ss_lo / half + EPS), lax.rsqrt(ss_hi / half + EPS))


def _split_pos(pos):
    return (pos & ~(POS_SPLIT - 1)).astype(F32), (pos & (POS_SPLIT - 1)).astype(F32)


def _attn_kernel(q_ref, k_ref, vt_ref, qw_ref, kw_ref, lam_ref, swc_ref, o_ref,
                 k1_s, k2_s, va_s, kpos_s, qpos_s, dist_s, q_s, bias_s, acc_s, m_s, l_s, accx_s,
                 *, tk, lam_init):
    tq = q_ref.shape[0]
    seq = k_ref.shape[0]
    nk = seq // tk
    nvar = tk // tq
    h = pl.program_id(1)
    qi = pl.program_id(2)
    lane = lax.broadcasted_iota(jnp.int32, (1, LANES), 1)
    lo_mask = lane < HALF
    slope = lax.bitcast_convert_type(jnp.full((1, LANES), (126 - h) << 23, jnp.int32), F32)

    @pl.when((pl.program_id(0) == 0) & (h == 0) & (qi == 0))
    def _():
        hi, lo = _split_pos(lax.broadcasted_iota(jnp.int32, (seq, LANES), 0))
        kpos_s[...] = jnp.where(lane == LANE_SHI, hi, jnp.where(
            lane == LANE_SLO, lo, jnp.where((lane >= LANE_SHIFT) & (lane <= LANE_TLO), 1.0, 0.0)))
        qpos_s[...] = jnp.where(lane == LANE_THI, -hi, jnp.where(
            lane == LANE_TLO, -lo, jnp.where((lane == LANE_SHI) | (lane == LANE_SLO), 1.0, 0.0)))
        rel = (lax.broadcasted_iota(jnp.int32, (tk, tq), 1) - lax.broadcasted_iota(jnp.int32, (tk, tq), 0))
        for v in range(nvar):
            dist_s[v] = jnp.abs(rel + v * tq).astype(F32)
        bias_s[0] = jnp.zeros((tk, tq), F32)

    @pl.when(qi == 0)
    def _():
        for c in range(nk):
            rows = slice(c * tk, (c + 1) * tk)
            kf = k_ref[rows, :].astype(F32)
            kn = kf * _half_rms_scale(kf, lo_mask) * kw_ref[...]
            pos = kpos_s[rows, :]
            k1_s[rows, :] = jnp.where(lo_mask, kn, pos).astype(BF16)
            k2_s[rows, :] = jnp.where(lo_mask, pltpu.roll(kn, HALF, 1), pos).astype(BF16)
        va_s[:LANES, :] = vt_ref[...]
        va_s[LANES:, :] = jnp.ones((ONES_ROWS, seq), BF16)
        for v in range(nvar):
            bias_s[1 + v] = dist_s[v] * (-slope[:, :1])

    def gain_max(w_ref, mask):
        return jnp.max(jnp.where(mask, jnp.abs(w_ref[...]), 0.0), axis=1, keepdims=True)

    m1 = math.sqrt(HALF) * gain_max(qw_ref, lo_mask) * gain_max(kw_ref, lo_mask)
    m2 = math.sqrt(HALF) * gain_max(qw_ref, ~lo_mask) * gain_max(kw_ref, ~lo_mask)

    qf = q_ref[...].astype(F32)
    qn = qf * _half_rms_scale(qf, lo_mask) * qw_ref[...] * (HALF ** -0.5)
    q1 = jnp.where(lo_mask, qn, 0.0)
    q2 = jnp.where(lo_mask, pltpu.roll(qn, HALF, 1), 0.0)
    q1m = jnp.where(lane == LANE_SHIFT, -m1, q1)
    q2m = jnp.where(lane == LANE_SHIFT, -m2, q2)
    pen = qpos_s[pl.ds(pl.multiple_of(qi * tq, tq), tq), :] * slope
    for v, sigma in enumerate((1.0, -1.0, 0.0)):
        q_s[v] = (q1m + sigma * pen).astype(BF16)
        q_s[3 + v] = (q2m + sigma * pen).astype(BF16)

    kjm = (qi * tq) // tk
    diag_tile = 1 + (qi * tq - kjm * tk) // tq

    for kj in range(nk):
        rows = slice(kj * tk, (kj + 1) * tk)
        diag = kjm == kj
        ver = jnp.where(diag, 2, jnp.where(kjm > kj, 0, 1))
        bias = bias_s[jnp.where(diag, diag_tile, 0)]
        vv = va_s[:, rows]
        p1 = jnp.exp(_dot_nt(k1_s[rows, :], q_s[ver]) + bias)
        p2 = jnp.exp(_dot_nt(k2_s[rows, :], q_s[3 + ver]) + bias)
        if kj == 0:
            acc_s[0] = _dot(vv, p1.astype(BF16))
            acc_s[1] = _dot(vv, p2.astype(BF16))
        else:
            acc_s[0] += _dot(vv, p1.astype(BF16))
            acc_s[1] += _dot(vv, p2.astype(BF16))

    lam_p = lam_ref[...]
    lam = (jnp.exp(jnp.sum(lam_p[0:1] * lam_p[1:2], axis=1, keepdims=True))
           - jnp.exp(jnp.sum(lam_p[2:3] * lam_p[3:4], axis=1, keepdims=True)) + lam_init)

    def finish(o1t, o2t):
        ot = o1t - lam * o2t
        var = jnp.mean(ot * ot, axis=0, keepdims=True)
        ot = ot * lax.rsqrt(var + EPS) * swc_ref[...] * (1.0 - lam_init)
        o_ref[...] = ot.T.astype(o_ref.dtype)

    a1 = acc_s[0]
    a2 = acc_s[1]
    l1 = a1[LANES:LANES + 1, :]
    l2 = a2[LANES:LANES + 1, :]
    finish(a1[:LANES] / l1, a2[:LANES] / l2)

    @pl.when(jnp.minimum(jnp.min(l1), jnp.min(l2)) < ROWSUM_MIN)
    def _():
        q_both = jnp.concatenate([q1, q2], axis=0).astype(BF16)
        m_s[...] = jnp.full(m_s.shape, -jnp.inf, F32)
        l_s[...] = jnp.zeros(l_s.shape, F32)
        accx_s[...] = jnp.zeros(accx_s.shape, F32)
        rel = (lax.broadcasted_iota(jnp.int32, (tq, tk), 0) - lax.broadcasted_iota(jnp.int32, (tq, tk), 1)
               + qi * tq).astype(F32)

        def exact_body(kj, carry):
            k0 = pl.multiple_of(kj * tk, tk)
            s = jnp.concatenate([_dot_nt(q_both[:tq], k1_s[pl.ds(k0, tk), :]),
                                 _dot_nt(q_both[tq:], k2_s[pl.ds(k0, tk), :])], axis=0)
            pen = jnp.abs(rel - lax.convert_element_type(kj * tk, F32)) * (-slope[:, :1])
            s = s + jnp.concatenate([pen, pen], axis=0)
            m_prev = m_s[...]
            m_new = jnp.maximum(m_prev, jnp.max(s, axis=1, keepdims=True))
            alpha = jnp.exp(m_prev - m_new)
            p = jnp.exp(s - m_new)
            l_s[...] = alpha * l_s[...] + jnp.sum(p, axis=1, keepdims=True)
            accx_s[...] = alpha * accx_s[...] + _dot_nt(p.astype(BF16), va_s[:LANES, pl.ds(k0, tk)])
            m_s[...] = m_new
            return carry

        lax.fori_loop(0, nk, exact_body, 0)
        acc = accx_s[...]
        l = l_s[...]
        finish((acc[:tq] / l[:tq]).T, (acc[tq:] / l[tq:]).T)


def _attn(pb, pvt, qn_w, kn_w, lam_p, subln_w, *, layer, batch, seq, tq=512, tk=512):
    n = batch * seq
    tq = min(tq, seq)
    tk = min(tk, seq)
    nq = seq // tq
    assert tk % tq == 0 and seq % tk == 0 and seq <= POS_SPLIT * POS_SPLIT
    lam_init = 0.8 - 0.6 * math.exp(-0.3 * layer)
    blocks = 2 * _nbytes((tq, LANES), BF16) + 2 * _nbytes((seq, LANES), BF16)
    scratch = (4 * _nbytes((seq, LANES), BF16) + 2 * _nbytes((seq, LANES), F32)
               + 6 * _nbytes((tq, LANES), BF16)
               + (1 + 2 * (tk // tq)) * _nbytes((tq, tk), F32) + 2 * _nbytes((tq, 2 * LANES), F32)
               + 3 * _nbytes((2 * tq, LANES), F32) + 4 * _nbytes((2 * tq, tk), F32))
    return pl.pallas_call(
        functools.partial(_attn_kernel, tk=tk, lam_init=lam_init),
        grid=(batch, HEADS, nq),
        in_specs=[
            pl.BlockSpec((None, tq, LANES), lambda b, h, i: (2 * HEADS + h, b * nq + i, 0)),
            pl.BlockSpec((None, seq, LANES), lambda b, h, i: (3 * HEADS + h, b, 0)),
            pl.BlockSpec((None, LANES, seq), lambda b, h, i: (HEADS + h, 0, b)),
            pl.BlockSpec((1, LANES), lambda b, h, i: (0, 0)),
            pl.BlockSpec((1, LANES), lambda b, h, i: (0, 0)),
            pl.BlockSpec((4, LANES // 2), lambda b, h, i: (0, 0)),
            pl.BlockSpec((LANES, 1), lambda b, h, i: (0, 0)),
        ],
        out_specs=pl.BlockSpec((None, tq, LANES), lambda b, h, i: (h, b * nq + i, 0)),
        out_shape=jax.ShapeDtypeStruct((HEADS, n, LANES), BF16),
        scratch_shapes=[
            pltpu.VMEM((seq, LANES), BF16),
            pltpu.VMEM((seq, LANES), BF16),
            pltpu.VMEM((LANES + ONES_ROWS, seq), BF16),
            pltpu.VMEM((seq, LANES), F32),
            pltpu.VMEM((seq, LANES), F32),
            pltpu.VMEM((tk // tq, tk, tq), F32),
            pltpu.VMEM((6, tq, LANES), BF16),
            pltpu.VMEM((1 + tk // tq, tk, tq), F32),
            pltpu.VMEM((2, LANES + ONES_ROWS, tq), F32),
            pltpu.VMEM((2 * tq, 1), F32),
            pltpu.VMEM((2 * tq, 1), F32),
            pltpu.VMEM((2 * tq, LANES), F32),
        ],
        compiler_params=pltpu.CompilerParams(
            dimension_semantics=("arbitrary", "arbitrary", "arbitrary"),
            vmem_limit_bytes=_vmem_limit(blocks, scratch)),
        name="diffattn",
    )(pb, pb, pvt, jnp.tile(qn_w, 2).reshape(1, LANES), jnp.tile(kn_w, 2).reshape(1, LANES),
      lam_p, subln_w.reshape(LANES, 1))


def _outproj_kernel(ohg_ref, oda_ref, x_ref, w_ref, o_ref, mix_s):
    for c in range(HEADS):
        mix_s[:, c * LANES:(c + 1) * LANES] = ohg_ref[c]
        mix_s[:, (HEADS + c) * LANES:(HEADS + c + 1) * LANES] = oda_ref[c]
    o_ref[...] = x_ref[...] + _dot(mix_s[...], w_ref[...])


def _outproj(ohg, oda, x, w, *, tm=512):
    n, d = x.shape
    width = w.shape[0]
    tm = min(tm, n)
    blocks = (2 * _nbytes((HEADS, tm, LANES), BF16) + 2 * _nbytes((tm, d), F32) + _nbytes((width, d), BF16))
    return pl.pallas_call(
        _outproj_kernel,
        grid=(n // tm,),
        in_specs=[
            pl.BlockSpec((HEADS, tm, LANES), lambda i: (0, i, 0)),
            pl.BlockSpec((HEADS, tm, LANES), lambda i: (0, i, 0)),
            pl.BlockSpec((tm, d), lambda i: (i, 0)),
            pl.BlockSpec((width, d), lambda i: (0, 0)),
        ],
        out_specs=pl.BlockSpec((tm, d), lambda i: (i, 0)),
        out_shape=jax.ShapeDtypeStruct((n, d), F32),
        scratch_shapes=[pltpu.VMEM((tm, width), BF16)],
        compiler_params=pltpu.CompilerParams(
            dimension_semantics=("parallel",),
            vmem_limit_bytes=_vmem_limit(blocks, _nbytes((tm, width), BF16))),
        name="outproj",
    )(ohg, oda, x, w)


def _ffn_kernel(x_ref, nw_ref, wg_ref, wu_ref, wd_ref, o_ref, h_s):
    @pl.when(pl.program_id(1) == 0)
    def _():
        x = x_ref[...]
        var = jnp.mean(x * x, axis=-1, keepdims=True)
        h_s[...] = (x * lax.rsqrt(var + EPS) * nw_ref[...]).astype(BF16)
        o_ref[...] = x

    h = h_s[...]
    g = _dot(h, wg_ref[...])
    u = _dot(h, wu_ref[...])
    o_ref[...] += _dot((g * _sigmoid(g) * u).astype(BF16), wd_ref[...])


def _ffn(x, nw, wg, wu, wd, *, tm=1024, tf=512):
    n, d = x.shape
    ff = wg.shape[1]
    tm = min(tm, n)
    blocks = (2 * _nbytes((tm, d), F32) + 3 * _nbytes((d, tf), BF16))
    scratch = _nbytes((tm, d), BF16) + 3 * _nbytes((tm, tf), F32)
    return pl.pallas_call(
        _ffn_kernel,
        grid=(n // tm, ff // tf),
        in_specs=[
            pl.BlockSpec((tm, d), lambda i, f: (i, 0)),
            pl.BlockSpec((1, d), lambda i, f: (0, 0)),
            pl.BlockSpec((d, tf), lambda i, f: (0, f)),
            pl.BlockSpec((d, tf), lambda i, f: (0, f)),
            pl.BlockSpec((tf, d), lambda i, f: (f, 0)),
        ],
        out_specs=pl.BlockSpec((tm, d), lambda i, f: (i, 0)),
        out_shape=jax.ShapeDtypeStruct((n, d), F32),
        scratch_shapes=[pltpu.VMEM((tm, d), BF16)],
        compiler_params=pltpu.CompilerParams(
            dimension_semantics=("parallel", "arbitrary"),
            vmem_limit_bytes=_vmem_limit(blocks, scratch)),
        name="ffn",
    )(x, nw.reshape(1, d), wg, wu, wd)


def kernel(x, norm_mix_w, w_in, hg_lb_logits, hg_onorm_w, da_qnorm_w, da_knorm_w, da_lambda,
           da_subln_w, w_out, norm_ffn_w, w_gate, w_up, w_down):
    batch, seq, d = x.shape
    depth = w_in.shape[0]
    hg = d // 2
    assert hg == HEADS * LANES and seq % CHUNK == 0
    xf = x.reshape(batch * seq, d).astype(F32)
    lb_logits = hg_lb_logits.astype(F32).reshape(2 * depth, hg)

    tn = hg
    w_in_bf = w_in.astype(BF16)
    per_tile = tn // 512
    w_in_t = _transposed_columns(w_in, tuple(3 * per_tile + c for c in range(per_tile))
                                 + tuple(7 * per_tile + c for c in range(per_tile)))

    for l in range(depth):
        pb, pz, pvt = _proj(xf, norm_mix_w[l], w_in_bf, w_in_t, l, (0, 4, 5, 6), (1, 2), tn=tn)
        ohg = _hgrn(pb, pvt, pz, lb_logits, hg_onorm_w[l], layer=l, batch=batch, seq=seq)
        oda = _attn(pb, pvt, da_qnorm_w[l], da_knorm_w[l], da_lambda[l].astype(F32), da_subln_w[l],
                    layer=l, batch=batch, seq=seq)
        x1 = _outproj(ohg, oda, xf, w_out[l].astype(BF16))
        xf = _ffn(x1, norm_ffn_w[l], w_gate[l].astype(BF16), w_up[l].astype(BF16),
                  w_down[l].astype(BF16))
    return xf.reshape(batch, seq, d).astype(x.dtype)
```

```python
import functools
import math

import jax
import jax.numpy as jnp
from jax import lax
from jax.experimental import pallas as pl
from jax.experimental.pallas import tpu as pltpu

F32 = jnp.float32
BF16 = jnp.bfloat16

LANES = 128
HEADS = 8
EPS = 1e-6
LB_FLOOR = 1e-30
LB_CEIL = 1.0 - 1e-6
CHUNK = 64
HGRN_GROUP = 2
EXP_CLAMP = 80.0
VMEM_CAP = 60 * 1024 * 1024
VMEM_SLACK = 12 * 1024 * 1024


def _vmem_limit(pipelined_bytes, scratch_bytes=0):
    return int(min(VMEM_CAP, 2 * pipelined_bytes + scratch_bytes + VMEM_SLACK))


def _nbytes(shape, dtype):
    return math.prod(shape) * jnp.dtype(dtype).itemsize


def _dot(a, b):
    return jnp.dot(a, b, preferred_element_type=F32)


def _dot_nt(a, b):
    return lax.dot_general(a, b, (((1,), (1,)), ((), ())), preferred_element_type=F32)


def _sigmoid(x):
    return 1.0 / (1.0 + jnp.exp(-x))


def _proj_kernel(x_ref, nw_ref, w_ref, wt_ref, ob_ref, oz_ref, ot_ref, h_s, *, nb, nz):
    j = pl.program_id(1)

    @pl.when(j == 0)
    def _():
        x = x_ref[...]
        var = jnp.mean(x * x, axis=-1, keepdims=True)
        h_s[...] = (x * lax.rsqrt(var + EPS) * nw_ref[...]).astype(BF16)

    def slabs(o_ref):
        r = _dot(h_s[...], w_ref[...])
        for c in range(o_ref.shape[0]):
            o_ref[c] = r[:, c * LANES:(c + 1) * LANES].astype(o_ref.dtype)

    @pl.when(j < nb)
    def _():
        slabs(ob_ref)

    @pl.when((j >= nb) & (j < nb + nz))
    def _():
        slabs(oz_ref)

    @pl.when(j >= nb + nz)
    def _():
        r = _dot_nt(wt_ref[...], h_s[...])
        for c in range(ot_ref.shape[0]):
            ot_ref[c] = r[c * LANES:(c + 1) * LANES, :].astype(ot_ref.dtype)


def _pick(j, values):
    out = values[-1]
    for idx in range(len(values) - 2, -1, -1):
        out = jnp.where(j <= idx, values[idx], out)
    return out


def _wt_kernel(w_ref, o_ref):
    o_ref[...] = w_ref[...].T.astype(o_ref.dtype)


def _transposed_columns(w, col_tiles, *, t=512):
    depth, d, _ = w.shape
    return pl.pallas_call(
        _wt_kernel,
        grid=(depth, len(col_tiles), d // t),
        in_specs=[pl.BlockSpec((None, t, t), lambda l, c, r: (l, r, _pick(c, col_tiles)))],
        out_specs=pl.BlockSpec((None, t, t), lambda l, c, r: (l, c, r)),
        out_shape=jax.ShapeDtypeStruct((depth, len(col_tiles) * t, d), BF16),
        compiler_params=pltpu.CompilerParams(
            dimension_semantics=("parallel", "parallel", "parallel"),
            vmem_limit_bytes=_vmem_limit(_nbytes((t, t), F32) + _nbytes((t, t), BF16), 2 * _nbytes((t, t), F32))),
        name="wtranspose",
    )(w)


def _proj(x, nw, w, wt, layer, tiles_b, tiles_z, *, tm=1024, tn=1024):
    n, d = x.shape
    tm = min(tm, n)
    nb, nz, nt = len(tiles_b), len(tiles_z), wt.shape[1] // tn
    assert nt * tn == wt.shape[1] and w.shape[2] % tn == 0
    per = tn // LANES
    main_tiles = tuple(tiles_b) + tuple(tiles_z)
    blocks = (_nbytes((tm, d), F32) + 2 * _nbytes((d, tn), BF16) + _nbytes((tm, tn), BF16)
              + _nbytes((tm, tn), F32) + _nbytes((tm, tn), BF16))
    return pl.pallas_call(
        functools.partial(_proj_kernel, nb=nb, nz=nz),
        grid=(n // tm, nb + nz + nt),
        in_specs=[
            pl.BlockSpec((tm, d), lambda i, j: (i, 0)),
            pl.BlockSpec((1, d), lambda i, j: (0, 0)),
            pl.BlockSpec((None, d, tn), lambda i, j: (layer, 0, _pick(j, main_tiles))),
            pl.BlockSpec((None, tn, d), lambda i, j: (layer, jnp.clip(j - nb - nz, 0, nt - 1), 0)),
        ],
        out_specs=[
            pl.BlockSpec((per, tm, LANES), lambda i, j: (jnp.minimum(j, nb - 1), i, 0)),
            pl.BlockSpec((per, tm, LANES), lambda i, j: (jnp.clip(j - nb, 0, nz - 1), i, 0)),
            pl.BlockSpec((per, LANES, tm), lambda i, j: (jnp.clip(j - nb - nz, 0, nt - 1), 0, i)),
        ],
        out_shape=[
            jax.ShapeDtypeStruct((nb * per, n, LANES), BF16),
            jax.ShapeDtypeStruct((nz * per, n, LANES), F32),
            jax.ShapeDtypeStruct((nt * per, LANES, n), BF16),
        ],
        scratch_shapes=[pltpu.VMEM((tm, d), BF16)],
        compiler_params=pltpu.CompilerParams(
            dimension_semantics=("parallel", "arbitrary"),
            vmem_limit_bytes=_vmem_limit(blocks, _nbytes((tm, d), BF16) + _nbytes((tm, tn), F32))),
        name="proj",
    )(x, nw.reshape(1, d), w, wt)


def _seg_cumsum(x, seg, rev):
    n = x.shape[0]
    pos = lax.broadcasted_iota(jnp.int32, x.shape, 0) & (seg - 1)
    s = 1
    while s < seg:
        if rev:
            x = x + jnp.where(pos < seg - s, pltpu.roll(x, n - s, 0), 0.0)
        else:
            x = x + jnp.where(pos >= s, pltpu.roll(x, s, 0), 0.0)
        s *= 2
    return x


def _rows_bcast(x, rows, reps):
    return jnp.concatenate([jnp.broadcast_to(x[r:r + 1, :], (reps, x.shape[1])) for r in rows], axis=0)


def _hgrn_kernel(q_ref, vt_ref, g_ref, zf_ref, zb_ref, lbl_ref, ow_ref, o_ref,
                 oacc_s, k_s, a_s, v_s, intra_s, *, layer, depth, tblk):
    seq = q_ref.shape[0]
    nblk = seq // tblk
    nchunk = tblk // CHUNK

    def lower_bound(direction):
        lg = lbl_ref[direction * depth:(direction + 1) * depth, :]
        e = jnp.exp(lg - jnp.max(lg, axis=0, keepdims=True))
        p = e / jnp.sum(e, axis=0, keepdims=True)
        lb = jnp.sum(p[:layer + 1], axis=0, keepdims=True) - p[0:1]
        lb = jnp.clip(lb, 0.0, LB_CEIL)
        return lb, jnp.maximum(lb, LB_FLOOR)

    row_c = lax.broadcasted_iota(jnp.int32, (CHUNK, LANES), 0)
    sr = lax.broadcasted_iota(jnp.int32, (tblk, tblk), 0)
    sc_ = lax.broadcasted_iota(jnp.int32, (tblk, tblk), 1)
    same_chunk = (sr // CHUNK) == (sc_ // CHUNK)
    own_lanes = (lax.broadcasted_iota(jnp.int32, (tblk, nchunk * LANES), 0) // CHUNK
                 == lax.broadcasted_iota(jnp.int32, (tblk, nchunk * LANES), 1) // LANES)

    bounds = (lower_bound(0), lower_bound(1))
    z_refs = (zf_ref, zb_ref)

    def fast_block(direction, slot, r0, st):
        rev = direction == 1
        lb, lbp = bounds[direction]
        causal = same_chunk & ((sc_ >= sr) if rev else (sc_ <= sr))
        q = q_ref[pl.ds(r0, tblk), :].astype(F32)
        vt = vt_ref[:, pl.ds(r0, tblk)]
        z = z_refs[direction][pl.ds(r0, tblk), :]
        sg = _sigmoid(z)
        lf = jnp.log(lbp + (1.0 - lb) * sg)
        k = (1.0 - lb) * (1.0 - sg) - (lbp - lb)
        a = _seg_cumsum(lf, CHUNK, rev)
        edge = (lambda c: c * CHUNK) if rev else (lambda c: c * CHUNK + CHUNK - 1)
        a_edge = _rows_bcast(a, [edge(c) for c in range(nchunk)], CHUNK)
        d = a - _rows_bcast(a, [c * CHUNK + CHUNK // 2 for c in range(nchunk)], CHUNK)
        overflow = jnp.max(jnp.abs(d)) > EXP_CLAMP
        e_mid = jnp.exp(jnp.clip(d, -EXP_CLAMP, EXP_CLAMP))
        qs = (q * e_mid).astype(BF16)
        ks = (k / e_mid).astype(BF16)
        qa = (q * jnp.exp(a)).astype(BF16)
        kb = (k * jnp.exp(a_edge - a)).astype(BF16)
        k_s[slot] = k
        a_s[slot] = a

        scores = jnp.where(causal, _dot_nt(qs, ks), 0.0).astype(BF16)
        intra_s[slot] = _dot_nt(scores, vt)
        upd = _dot(vt, jnp.where(own_lanes, jnp.concatenate([kb] * nchunk, axis=1), 0.0))
        inter = [None] * nchunk
        for c in (range(nchunk - 1, -1, -1) if rev else range(nchunk)):
            lo = c * CHUNK
            inter[c] = _dot_nt(qa[lo:lo + CHUNK], st.astype(BF16))
            st = st * jnp.exp(a[edge(c):edge(c) + 1, :]) + upd[:, c * LANES:(c + 1) * LANES]
        return jnp.concatenate(inter, axis=0), overflow, st

    def exact_intra(direction, slot, r0):
        rev = direction == 1
        v_s[...] = vt_ref[:, pl.ds(r0, tblk)].astype(F32).T
        for c in range(nchunk):
            lo = c * CHUNK
            q_c = q_ref[pl.ds(r0 + lo, CHUNK), :].astype(F32)
            a_c = a_s[slot, lo:lo + CHUNK, :]

            def pair(s, acc):
                w = q_c * k_s[slot, pl.ds(lo + s, 1), :] * jnp.exp(
                    jnp.minimum(a_c - a_s[slot, pl.ds(lo + s, 1), :], 0.0))
                r = jnp.sum(w, axis=1, keepdims=True)
                m = (row_c <= s) if rev else (row_c >= s)
                return acc + jnp.where(m, r, 0.0) * v_s[pl.ds(lo + s, 1), :]

            intra_s[slot, lo:lo + CHUNK, :] = lax.fori_loop(
                0, CHUNK, pair, jnp.zeros((CHUNK, LANES), F32))

    def emit(r0, o):
        var = jnp.mean(o * o, axis=-1, keepdims=True)
        gate = g_ref[pl.ds(r0, tblk), :].astype(F32)
        o = o * lax.rsqrt(var + EPS) * ow_ref[...] * (gate * _sigmoid(gate))
        o_ref[pl.ds(r0, tblk), :] = o.astype(o_ref.dtype)

    def sweep(second_half):
        def body(i, carry):
            st = list(carry)
            work = []
            for u in range(HGRN_GROUP):
                blk = i * HGRN_GROUP + u
                for direction in (0, 1):
                    r0 = pl.multiple_of((blk if direction == 0 else nblk - 1 - blk) * tblk, tblk)
                    slot = 2 * u + direction
                    inter, overflow, st[direction] = fast_block(direction, slot, r0, st[direction])
                    work.append((direction, slot, r0, inter, overflow))

            for direction, slot, r0, _, overflow in work:
                @pl.when(overflow)
                def _(direction=direction, slot=slot, r0=r0):
                    exact_intra(direction, slot, r0)

            for _, slot, r0, inter, _ in work:
                o = inter + intra_s[slot]
                if second_half:
                    emit(r0, o + oacc_s[pl.ds(r0, tblk), :])
                else:
                    oacc_s[pl.ds(r0, tblk), :] = o
            return tuple(st)
        return body

    zero = jnp.zeros((LANES, LANES), F32)
    steps = nblk // HGRN_GROUP
    carry = lax.fori_loop(0, steps // 2, sweep(False), (zero, zero))
    lax.fori_loop(steps // 2, steps, sweep(True), carry)


def _hgrn(pb, pvt, pz, lb_logits, onorm_w, *, layer, batch, seq, tblk=256):
    depth = lb_logits.shape[0] // 2
    n = batch * seq
    tblk = min(tblk, seq)

    def slab(base):
        return pl.BlockSpec((None, seq, LANES), lambda b, h: (base + h, b, 0))

    assert seq % (2 * HGRN_GROUP * tblk) == 0
    blocks = 4 * _nbytes((seq, LANES), BF16) + 2 * _nbytes((seq, LANES), F32)
    scratch = _nbytes((seq, LANES), F32) + (1 + 6 * HGRN_GROUP) * _nbytes((tblk, LANES), F32)
    return pl.pallas_call(
        functools.partial(_hgrn_kernel, layer=layer, depth=depth, tblk=tblk),
        grid=(batch, HEADS),
        in_specs=[
            slab(0),
            pl.BlockSpec((None, LANES, seq), lambda b, h: (h, 0, b)),
            slab(HEADS),
            pl.BlockSpec((None, seq, LANES), lambda b, h: (h, b, 0)),
            pl.BlockSpec((None, seq, LANES), lambda b, h: (HEADS + h, b, 0)),
            pl.BlockSpec((2 * depth, LANES), lambda b, h: (0, h)),
            pl.BlockSpec((1, LANES), lambda b, h: (0, 0)),
        ],
        out_specs=pl.BlockSpec((None, seq, LANES), lambda b, h: (h, b, 0)),
        out_shape=jax.ShapeDtypeStruct((HEADS, n, LANES), BF16),
        scratch_shapes=[
            pltpu.VMEM((seq, LANES), F32),
            pltpu.VMEM((2 * HGRN_GROUP, tblk, LANES), F32),
            pltpu.VMEM((2 * HGRN_GROUP, tblk, LANES), F32),
            pltpu.VMEM((tblk, LANES), F32),
            pltpu.VMEM((2 * HGRN_GROUP, tblk, LANES), F32),
        ],
        compiler_params=pltpu.CompilerParams(
            dimension_semantics=("parallel", "parallel"),
            vmem_limit_bytes=_vmem_limit(blocks, scratch)),
        name="hgrn",
    )(pb, pvt, pb, pz, pz, lb_logits, onorm_w.reshape(1, LANES))


HALF = LANES // 2
LANE_SHIFT = HALF
LANE_THI = HALF + 1
LANE_TLO = HALF + 2
LANE_SHI = HALF + 3
LANE_SLO = HALF + 4
POS_SPLIT = 256
ROWSUM_MIN = 1e-24
ATTN_GROUP = 2
ONES_ROWS = 16


def _half_rms_scale(x, lo_mask):
    sq = x * x
    half = x.shape[1] // 2
    ss_lo = jnp.sum(jnp.where(lo_mask, sq, 0.0), axis=1, keepdims=True)
    ss_hi = jnp.sum(jnp.where(lo_mask, 0.0, sq), axis=1, keepdims=True)
    return jnp.where(lo_mask, lax.rsqrt(ss_lo / half + EPS), lax.rsqrt(ss_hi / half + EPS))


def _split_pos(pos):
    return (pos & ~(POS_SPLIT - 1)).astype(F32), (pos & (POS_SPLIT - 1)).astype(F32)


def _attn_kernel(q_ref, k_ref, vt_ref, qw_ref, kw_ref, lam_ref, swc_ref, o_ref,
                 k1_s, k2_s, va_s, kpos_s, qpos_s, dist_s, q_s, bias_s, acc_s, m_s, l_s, accx_s,
                 *, tk, lam_init):
    tq = q_ref.shape[0] // ATTN_GROUP
    seq = k_ref.shape[0]
    nk = seq // tk
    nvar = tk // tq
    h = pl.program_id(1)
    qi = pl.program_id(2)
    lane = lax.broadcasted_iota(jnp.int32, (1, LANES), 1)
    lo_mask = lane < HALF
    slope = lax.bitcast_convert_type(jnp.full((1, LANES), (126 - h) << 23, jnp.int32), F32)

    @pl.when((pl.program_id(0) == 0) & (h == 0) & (qi == 0))
    def _():
        hi, lo = _split_pos(lax.broadcasted_iota(jnp.int32, (seq, LANES), 0))
        kpos_s[...] = jnp.where(lane == LANE_SHI, hi, jnp.where(
            lane == LANE_SLO, lo, jnp.where((lane >= LANE_SHIFT) & (lane <= LANE_TLO), 1.0, 0.0)))
        qpos_s[...] = jnp.where(lane == LANE_THI, -hi, jnp.where(
            lane == LANE_TLO, -lo, jnp.where((lane == LANE_SHI) | (lane == LANE_SLO), 1.0, 0.0)))
        rel = (lax.broadcasted_iota(jnp.int32, (tk, tq), 1) - lax.broadcasted_iota(jnp.int32, (tk, tq), 0))
        for v in range(nvar):
            dist_s[v] = jnp.abs(rel + v * tq).astype(F32)
        bias_s[0] = jnp.zeros((tk, tq), F32)

    @pl.when(qi == 0)
    def _():
        for c in range(nk):
            rows = slice(c * tk, (c + 1) * tk)
            kf = k_ref[rows, :].astype(F32)
            kn = kf * _half_rms_scale(kf, lo_mask) * kw_ref[...]
            pos = kpos_s[rows, :]
            k1_s[rows, :] = jnp.where(lo_mask, kn, pos).astype(BF16)
            k2_s[rows, :] = jnp.where(lo_mask, pltpu.roll(kn, HALF, 1), pos).astype(BF16)
        va_s[:LANES, :] = vt_ref[...]
        va_s[LANES:, :] = jnp.ones((ONES_ROWS, seq), BF16)
        for v in range(nvar):
            bias_s[1 + v] = dist_s[v] * (-slope[:, :1])

    def gain_max(w_ref, mask):
        return jnp.max(jnp.where(mask, jnp.abs(w_ref[...]), 0.0), axis=1, keepdims=True)

    m1 = math.sqrt(HALF) * gain_max(qw_ref, lo_mask) * gain_max(kw_ref, lo_mask)
    m2 = math.sqrt(HALF) * gain_max(qw_ref, ~lo_mask) * gain_max(kw_ref, ~lo_mask)

    lam_p = lam_ref[...]
    lam = (jnp.exp(jnp.sum(lam_p[0:1] * lam_p[1:2], axis=1, keepdims=True))
           - jnp.exp(jnp.sum(lam_p[2:3] * lam_p[3:4], axis=1, keepdims=True)) + lam_init)

    work = []
    for u in range(ATTN_GROUP):
        qb = qi * ATTN_GROUP + u
        qf = q_ref[u * tq:(u + 1) * tq, :].astype(F32)
        qn = qf * _half_rms_scale(qf, lo_mask) * qw_ref[...] * (HALF ** -0.5)
        q1 = jnp.where(lo_mask, qn, 0.0)
        q2 = jnp.where(lo_mask, pltpu.roll(qn, HALF, 1), 0.0)
        q1m = jnp.where(lane == LANE_SHIFT, -m1, q1)
        q2m = jnp.where(lane == LANE_SHIFT, -m2, q2)
        pen = qpos_s[pl.ds(pl.multiple_of(qb * tq, tq), tq), :] * slope
        for v, sigma in enumerate((1.0, -1.0, 0.0)):
            q_s[6 * u + v] = (q1m + sigma * pen).astype(BF16)
            q_s[6 * u + 3 + v] = (q2m + sigma * pen).astype(BF16)
        work.append((u, qb, q1, q2))

    for u, qb, _, _ in work:
        kjm = (qb * tq) // tk
        diag_tile = 1 + (qb * tq - kjm * tk) // tq
        for kj in range(nk):
            rows = slice(kj * tk, (kj + 1) * tk)
            diag = kjm == kj
            ver = jnp.where(diag, 2, jnp.where(kjm > kj, 0, 1))
            bias = bias_s[jnp.where(diag, diag_tile, 0)]
            vv = va_s[:, rows]
            p1 = jnp.exp(_dot_nt(k1_s[rows, :], q_s[6 * u + ver]) + bias)
            p2 = jnp.exp(_dot_nt(k2_s[rows, :], q_s[6 * u + 3 + ver]) + bias)
            if kj == 0:
                acc_s[2 * u] = _dot(vv, p1.astype(BF16))
                acc_s[2 * u + 1] = _dot(vv, p2.astype(BF16))
            else:
                acc_s[2 * u] += _dot(vv, p1.astype(BF16))
                acc_s[2 * u + 1] += _dot(vv, p2.astype(BF16))

    def finish(u, o1t, o2t):
        ot = o1t - lam * o2t
        var = jnp.mean(ot * ot, axis=0, keepdims=True)
        ot = ot * lax.rsqrt(var + EPS) * swc_ref[...] * (1.0 - lam_init)
        o_ref[u * tq:(u + 1) * tq, :] = ot.T.astype(o_ref.dtype)

    low = []
    for u, _, _, _ in work:
        a1 = acc_s[2 * u]
        a2 = acc_s[2 * u + 1]
        l1 = a1[LANES:LANES + 1, :]
        l2 = a2[LANES:LANES + 1, :]
        finish(u, a1[:LANES] / l1, a2[:LANES] / l2)
        low.append(jnp.minimum(jnp.min(l1), jnp.min(l2)))

    for (u, qb, q1, q2), lmin in zip(work, low):
        @pl.when(lmin < ROWSUM_MIN)
        def _(u=u, qb=qb, q1=q1, q2=q2):
            q_both = jnp.concatenate([q1, q2], axis=0).astype(BF16)
            m_s[...] = jnp.full(m_s.shape, -jnp.inf, F32)
            l_s[...] = jnp.zeros(l_s.shape, F32)
            accx_s[...] = jnp.zeros(accx_s.shape, F32)
            rel = (lax.broadcasted_iota(jnp.int32, (tq, tk), 0) - lax.broadcasted_iota(jnp.int32, (tq, tk), 1)
                   + qb * tq).astype(F32)

            def exact_body(kj, carry):
                k0 = pl.multiple_of(kj * tk, tk)
                s = jnp.concatenate([_dot_nt(q_both[:tq], k1_s[pl.ds(k0, tk), :]),
                                     _dot_nt(q_both[tq:], k2_s[pl.ds(k0, tk), :])], axis=0)
                pen = jnp.abs(rel - lax.convert_element_type(kj * tk, F32)) * (-slope[:, :1])
                s = s + jnp.concatenate([pen, pen], axis=0)
                m_prev = m_s[...]
                m_new = jnp.maximum(m_prev, jnp.max(s, axis=1, keepdims=True))
                alpha = jnp.exp(m_prev - m_new)
                p = jnp.exp(s - m_new)
                l_s[...] = alpha * l_s[...] + jnp.sum(p, axis=1, keepdims=True)
                accx_s[...] = alpha * accx_s[...] + _dot_nt(p.astype(BF16), va_s[:LANES, pl.ds(k0, tk)])
                m_s[...] = m_new
                return carry

            lax.fori_loop(0, nk, exact_body, 0)
            acc = accx_s[...]
            l = l_s[...]
            finish(u, (acc[:tq] / l[:tq]).T, (acc[tq:] / l[tq:]).T)


def _attn(pb, pvt, qn_w, kn_w, lam_p, subln_w, *, layer, batch, seq, tq=512, tk=512):
    n = batch * seq
    tq = min(tq, seq)
    tk = min(tk, seq)
    nq = seq // (ATTN_GROUP * tq)
    assert tk % tq == 0 and seq % tk == 0 and seq % (ATTN_GROUP * tq) == 0 and seq <= POS_SPLIT * POS_SPLIT
    lam_init = 0.8 - 0.6 * math.exp(-0.3 * layer)
    blocks = 2 * _nbytes((ATTN_GROUP * tq, LANES), BF16) + 2 * _nbytes((seq, LANES), BF16)
    scratch = (3 * _nbytes((seq, LANES), BF16) + _nbytes((ONES_ROWS, seq), BF16) + 2 * _nbytes((seq, LANES), F32)
               + 6 * ATTN_GROUP * _nbytes((tq, LANES), BF16)
               + (1 + 2 * (tk // tq)) * _nbytes((tk, tq), F32)
               + 2 * ATTN_GROUP * _nbytes((LANES + ONES_ROWS, tq), F32)
               + 3 * _nbytes((2 * tq, LANES), F32) + 4 * _nbytes((tk, tq), F32))
    return pl.pallas_call(
        functools.partial(_attn_kernel, tk=tk, lam_init=lam_init),
        grid=(batch, HEADS, nq),
        in_specs=[
            pl.BlockSpec((None, ATTN_GROUP * tq, LANES), lambda b, h, i: (2 * HEADS + h, b * nq + i, 0)),
            pl.BlockSpec((None, seq, LANES), lambda b, h, i: (3 * HEADS + h, b, 0)),
            pl.BlockSpec((None, LANES, seq), lambda b, h, i: (HEADS + h, 0, b)),
            pl.BlockSpec((1, LANES), lambda b, h, i: (0, 0)),
            pl.BlockSpec((1, LANES), lambda b, h, i: (0, 0)),
            pl.BlockSpec((4, LANES // 2), lambda b, h, i: (0, 0)),
            pl.BlockSpec((LANES, 1), lambda b, h, i: (0, 0)),
        ],
        out_specs=pl.BlockSpec((None, ATTN_GROUP * tq, LANES), lambda b, h, i: (h, b * nq + i, 0)),
        out_shape=jax.ShapeDtypeStruct((HEADS, n, LANES), BF16),
        scratch_shapes=[
            pltpu.VMEM((seq, LANES), BF16),
            pltpu.VMEM((seq, LANES), BF16),
            pltpu.VMEM((LANES + ONES_ROWS, seq), BF16),
            pltpu.VMEM((seq, LANES), F32),
            pltpu.VMEM((seq, LANES), F32),
            pltpu.VMEM((tk // tq, tk, tq), F32),
            pltpu.VMEM((6 * ATTN_GROUP, tq, LANES), BF16),
            pltpu.VMEM((1 + tk // tq, tk, tq), F32),
            pltpu.VMEM((2 * ATTN_GROUP, LANES + ONES_ROWS, tq), F32),
            pltpu.VMEM((2 * tq, 1), F32),
            pltpu.VMEM((2 * tq, 1), F32),
            pltpu.VMEM((2 * tq, LANES), F32),
        ],
        compiler_params=pltpu.CompilerParams(
            dimension_semantics=("arbitrary", "arbitrary", "arbitrary"),
            vmem_limit_bytes=_vmem_limit(blocks, scratch)),
        name="diffattn",
    )(pb, pb, pvt, jnp.tile(qn_w, 2).reshape(1, LANES), jnp.tile(kn_w, 2).reshape(1, LANES),
      lam_p, subln_w.reshape(LANES, 1))


def _outproj_kernel(ohg_ref, oda_ref, x_ref, w_ref, o_ref, mix_s):
    for c in range(HEADS):
        mix_s[:, c * LANES:(c + 1) * LANES] = ohg_ref[c]
        mix_s[:, (HEADS + c) * LANES:(HEADS + c + 1) * LANES] = oda_ref[c]
    o_ref[...] = x_ref[...] + _dot(mix_s[...], w_ref[...])


def _outproj(ohg, oda, x, w, layer, *, tm=512):
    n, d = x.shape
    width = w.shape[1]
    tm = min(tm, n)
    blocks = (2 * _nbytes((HEADS, tm, LANES), BF16) + 2 * _nbytes((tm, d), F32) + _nbytes((width, d), BF16))
    return pl.pallas_call(
        _outproj_kernel,
        grid=(n // tm,),
        in_specs=[
            pl.BlockSpec((HEADS, tm, LANES), lambda i: (0, i, 0)),
            pl.BlockSpec((HEADS, tm, LANES), lambda i: (0, i, 0)),
            pl.BlockSpec((tm, d), lambda i: (i, 0)),
            pl.BlockSpec((None, width, d), lambda i: (layer, 0, 0)),
        ],
        out_specs=pl.BlockSpec((tm, d), lambda i: (i, 0)),
        out_shape=jax.ShapeDtypeStruct((n, d), F32),
        scratch_shapes=[pltpu.VMEM((tm, width), BF16)],
        compiler_params=pltpu.CompilerParams(
            dimension_semantics=("parallel",),
            vmem_limit_bytes=_vmem_limit(blocks, _nbytes((tm, width), BF16))),
        name="outproj",
    )(ohg, oda, x, w)


def _ffn_kernel(x_ref, nw_ref, wg_ref, wu_ref, wd_ref, o_ref, h_s):
    @pl.when(pl.program_id(1) == 0)
    def _():
        x = x_ref[...]
        var = jnp.mean(x * x, axis=-1, keepdims=True)
        h_s[...] = (x * lax.rsqrt(var + EPS) * nw_ref[...]).astype(BF16)
        o_ref[...] = x

    h = h_s[...]
    g = _dot(h, wg_ref[...])
    u = _dot(h, wu_ref[...])
    o_ref[...] += _dot((g * _sigmoid(g) * u).astype(BF16), wd_ref[...])


def _ffn(x, nw, wg, wu, wd, layer, *, tm=1024, tf=512):
    n, d = x.shape
    ff = wg.shape[2]
    tm = min(tm, n)
    blocks = (2 * _nbytes((tm, d), F32) + 3 * _nbytes((d, tf), BF16))
    scratch = _nbytes((tm, d), BF16) + 3 * _nbytes((tm, tf), F32)
    return pl.pallas_call(
        _ffn_kernel,
        grid=(n // tm, ff // tf),
        in_specs=[
            pl.BlockSpec((tm, d), lambda i, f: (i, 0)),
            pl.BlockSpec((1, d), lambda i, f: (0, 0)),
            pl.BlockSpec((None, d, tf), lambda i, f: (layer, 0, f)),
            pl.BlockSpec((None, d, tf), lambda i, f: (layer, 0, f)),
            pl.BlockSpec((None, tf, d), lambda i, f: (layer, f, 0)),
        ],
        out_specs=pl.BlockSpec((tm, d), lambda i, f: (i, 0)),
        out_shape=jax.ShapeDtypeStruct((n, d), F32),
        scratch_shapes=[pltpu.VMEM((tm, d), BF16)],
        compiler_params=pltpu.CompilerParams(
            dimension_semantics=("parallel", "arbitrary"),
            vmem_limit_bytes=_vmem_limit(blocks, scratch)),
        name="ffn",
    )(x, nw.reshape(1, d), wg, wu, wd)


def kernel(x, norm_mix_w, w_in, hg_lb_logits, hg_onorm_w, da_qnorm_w, da_knorm_w, da_lambda,
           da_subln_w, w_out, norm_ffn_w, w_gate, w_up, w_down):
    batch, seq, d = x.shape
    depth = w_in.shape[0]
    hg = d // 2
    assert hg == HEADS * LANES and seq % CHUNK == 0
    xf = x.reshape(batch * seq, d).astype(F32)
    lb_logits = hg_lb_logits.astype(F32).reshape(2 * depth, hg)

    tn = hg
    w_in_bf = w_in.astype(BF16)
    per_tile = tn // 512
    w_in_t = _transposed_columns(w_in, tuple(3 * per_tile + c for c in range(per_tile))
                                 + tuple(7 * per_tile + c for c in range(per_tile)))
    w_out_bf, w_gate_bf, w_up_bf, w_down_bf = (w.astype(BF16) for w in (w_out, w_gate, w_up, w_down))

    for l in range(depth):
        pb, pz, pvt = _proj(xf, norm_mix_w[l], w_in_bf, w_in_t, l, (0, 4, 5, 6), (1, 2), tn=tn)
        ohg = _hgrn(pb, pvt, pz, lb_logits, hg_onorm_w[l], layer=l, batch=batch, seq=seq)
        oda = _attn(pb, pvt, da_qnorm_w[l], da_knorm_w[l], da_lambda[l].astype(F32), da_subln_w[l],
                    layer=l, batch=batch, seq=seq)
        x1 = _outproj(ohg, oda, xf, w_out_bf, l)
        xf = _ffn(x1, norm_ffn_w[l], w_gate_bf, w_up_bf, w_down_bf, l)
    return xf.reshape(batch, seq, d).astype(x.dtype)
```

```python
import functools
import math

import jax
import jax.numpy as jnp
from jax import lax
from jax.experimental import pallas as pl
from jax.experimental.pallas import tpu as pltpu

F32 = jnp.float32
BF16 = jnp.bfloat16

LANES = 128
HEADS = 8
EPS = 1e-6
LB_FLOOR = 1e-30
LB_CEIL = 1.0 - 1e-6
CHUNK = 64
HGRN_GROUP = 2
EXP_CLAMP = 80.0
VMEM_CAP = 60 * 1024 * 1024
VMEM_SLACK = 12 * 1024 * 1024


def _vmem_limit(pipelined_bytes, scratch_bytes=0):
    return int(min(VMEM_CAP, 2 * pipelined_bytes + scratch_bytes + VMEM_SLACK))


def _nbytes(shape, dtype):
    return math.prod(shape) * jnp.dtype(dtype).itemsize


def _dot(a, b):
    return jnp.dot(a, b, preferred_element_type=F32)


def _dot_nt(a, b):
    return lax.dot_general(a, b, (((1,), (1,)), ((), ())), preferred_element_type=F32)


def _sigmoid(x):
    return 1.0 / (1.0 + jnp.exp(-x))


def _proj_kernel(x_ref, nw_ref, w_ref, wt_ref, ob_ref, oz_ref, ot_ref, h_s, *, nb, nz):
    j = pl.program_id(1)

    @pl.when(j == 0)
    def _():
        x = x_ref[...]
        var = jnp.mean(x * x, axis=-1, keepdims=True)
        h_s[...] = (x * lax.rsqrt(var + EPS) * nw_ref[...]).astype(BF16)

    def slabs(o_ref):
        r = _dot(h_s[...], w_ref[...])
        for c in range(o_ref.shape[0]):
            o_ref[c] = r[:, c * LANES:(c + 1) * LANES].astype(o_ref.dtype)

    @pl.when(j < nb)
    def _():
        slabs(ob_ref)

    @pl.when((j >= nb) & (j < nb + nz))
    def _():
        slabs(oz_ref)

    @pl.when(j >= nb + nz)
    def _():
        r = _dot_nt(wt_ref[...], h_s[...])
        for c in range(ot_ref.shape[0]):
            ot_ref[c] = r[c * LANES:(c + 1) * LANES, :].astype(ot_ref.dtype)


def _pick(j, values):
    out = values[-1]
    for idx in range(len(values) - 2, -1, -1):
        out = jnp.where(j <= idx, values[idx], out)
    return out


def _wt_kernel(w_ref, o_ref):
    o_ref[...] = w_ref[...].T.astype(o_ref.dtype)


def _transposed_columns(w, col_tiles, *, t=512):
    depth, d, _ = w.shape
    return pl.pallas_call(
        _wt_kernel,
        grid=(depth, len(col_tiles), d // t),
        in_specs=[pl.BlockSpec((None, t, t), lambda l, c, r: (l, r, _pick(c, col_tiles)))],
        out_specs=pl.BlockSpec((None, t, t), lambda l, c, r: (l, c, r)),
        out_shape=jax.ShapeDtypeStruct((depth, len(col_tiles) * t, d), BF16),
        compiler_params=pltpu.CompilerParams(
            dimension_semantics=("parallel", "parallel", "parallel"),
            vmem_limit_bytes=_vmem_limit(_nbytes((t, t), F32) + _nbytes((t, t), BF16), 2 * _nbytes((t, t), F32))),
        name="wtranspose",
    )(w)


def _proj(x, nw, w, wt, layer, tiles_b, tiles_z, *, tm=1024, tn=1024):
    n, d = x.shape
    tm = min(tm, n)
    nb, nz, nt = len(tiles_b), len(tiles_z), wt.shape[1] // tn
    assert nt * tn == wt.shape[1] and w.shape[2] % tn == 0
    per = tn // LANES
    main_tiles = tuple(tiles_b) + tuple(tiles_z)
    blocks = (_nbytes((tm, d), F32) + 2 * _nbytes((d, tn), BF16) + _nbytes((tm, tn), BF16)
              + _nbytes((tm, tn), F32) + _nbytes((tm, tn), BF16))
    return pl.pallas_call(
        functools.partial(_proj_kernel, nb=nb, nz=nz),
        grid=(n // tm, nb + nz + nt),
        in_specs=[
            pl.BlockSpec((tm, d), lambda i, j: (jnp.minimum(i + jnp.minimum(j, 1), n // tm - 1), 0)),
            pl.BlockSpec((1, d), lambda i, j: (0, 0)),
            pl.BlockSpec((None, d, tn), lambda i, j: (layer, 0, _pick(j, main_tiles))),
            pl.BlockSpec((None, tn, d), lambda i, j: (layer, jnp.clip(j - nb - nz, 0, nt - 1), 0)),
        ],
        out_specs=[
            pl.BlockSpec((per, tm, LANES), lambda i, j: (jnp.minimum(j, nb - 1), i, 0)),
            pl.BlockSpec((per, tm, LANES), lambda i, j: (jnp.clip(j - nb, 0, nz - 1), i, 0)),
            pl.BlockSpec((per, LANES, tm), lambda i, j: (jnp.clip(j - nb - nz, 0, nt - 1), 0, i)),
        ],
        out_shape=[
            jax.ShapeDtypeStruct((nb * per, n, LANES), BF16),
            jax.ShapeDtypeStruct((nz * per, n, LANES), F32),
            jax.ShapeDtypeStruct((nt * per, LANES, n), BF16),
        ],
        scratch_shapes=[pltpu.VMEM((tm, d), BF16)],
        compiler_params=pltpu.CompilerParams(
            dimension_semantics=("parallel", "arbitrary"),
            vmem_limit_bytes=_vmem_limit(blocks, _nbytes((tm, d), BF16) + _nbytes((tm, tn), F32))),
        name="proj",
    )(x, nw.reshape(1, d), w, wt)


def _seg_cumsum(x, seg, rev):
    n = x.shape[0]
    pos = lax.broadcasted_iota(jnp.int32, x.shape, 0) & (seg - 1)
    s = 1
    while s < seg:
        if rev:
            x = x + jnp.where(pos < seg - s, pltpu.roll(x, n - s, 0), 0.0)
        else:
            x = x + jnp.where(pos >= s, pltpu.roll(x, s, 0), 0.0)
        s *= 2
    return x


def _rows_bcast(x, rows, reps):
    return jnp.concatenate([jnp.broadcast_to(x[r:r + 1, :], (reps, x.shape[1])) for r in rows], axis=0)


def _hgrn_kernel(q_ref, vt_ref, g_ref, zf_ref, zb_ref, lbl_ref, ow_ref, o_ref,
                 oacc_s, k_s, a_s, v_s, intra_s, *, layer, depth, tblk):
    seq = q_ref.shape[0]
    nblk = seq // tblk
    nchunk = tblk // CHUNK

    def lower_bound(direction):
        lg = lbl_ref[direction * depth:(direction + 1) * depth, :]
        e = jnp.exp(lg - jnp.max(lg, axis=0, keepdims=True))
        p = e / jnp.sum(e, axis=0, keepdims=True)
        lb = jnp.sum(p[:layer + 1], axis=0, keepdims=True) - p[0:1]
        lb = jnp.clip(lb, 0.0, LB_CEIL)
        return lb, jnp.maximum(lb, LB_FLOOR)

    row_c = lax.broadcasted_iota(jnp.int32, (CHUNK, LANES), 0)
    sr = lax.broadcasted_iota(jnp.int32, (tblk, tblk), 0)
    sc_ = lax.broadcasted_iota(jnp.int32, (tblk, tblk), 1)
    same_chunk = (sr // CHUNK) == (sc_ // CHUNK)
    own_lanes = (lax.broadcasted_iota(jnp.int32, (tblk, nchunk * LANES), 0) // CHUNK
                 == lax.broadcasted_iota(jnp.int32, (tblk, nchunk * LANES), 1) // LANES)

    bounds = (lower_bound(0), lower_bound(1))
    z_refs = (zf_ref, zb_ref)

    def fast_block(direction, slot, r0, st):
        rev = direction == 1
        lb, lbp = bounds[direction]
        causal = same_chunk & ((sc_ >= sr) if rev else (sc_ <= sr))
        q = q_ref[pl.ds(r0, tblk), :].astype(F32)
        vt = vt_ref[:, pl.ds(r0, tblk)]
        z = z_refs[direction][pl.ds(r0, tblk), :]
        sg = _sigmoid(z)
        lf = jnp.log(lbp + (1.0 - lb) * sg)
        k = (1.0 - lb) * (1.0 - sg) - (lbp - lb)
        a = _seg_cumsum(lf, CHUNK, rev)
        edge = (lambda c: c * CHUNK) if rev else (lambda c: c * CHUNK + CHUNK - 1)
        a_edge = _rows_bcast(a, [edge(c) for c in range(nchunk)], CHUNK)
        d = a - _rows_bcast(a, [c * CHUNK + CHUNK // 2 for c in range(nchunk)], CHUNK)
        overflow = jnp.max(jnp.abs(d)) > EXP_CLAMP
        e_mid = jnp.exp(jnp.clip(d, -EXP_CLAMP, EXP_CLAMP))
        qs = (q * e_mid).astype(BF16)
        ks = (k / e_mid).astype(BF16)
        qa = (q * jnp.exp(a)).astype(BF16)
        kb = (k * jnp.exp(a_edge - a)).astype(BF16)
        k_s[slot] = k
        a_s[slot] = a

        scores = jnp.where(causal, _dot_nt(qs, ks), 0.0).astype(BF16)
        intra_s[slot] = _dot_nt(scores, vt)
        upd = _dot(vt, jnp.where(own_lanes, jnp.concatenate([kb] * nchunk, axis=1), 0.0))
        inter = [None] * nchunk
        for c in (range(nchunk - 1, -1, -1) if rev else range(nchunk)):
            lo = c * CHUNK
            inter[c] = _dot_nt(qa[lo:lo + CHUNK], st.astype(BF16))
            st = st * jnp.exp(a[edge(c):edge(c) + 1, :]) + upd[:, c * LANES:(c + 1) * LANES]
        return jnp.concatenate(inter, axis=0), overflow, st

    def exact_intra(direction, slot, r0):
        rev = direction == 1
        v_s[...] = vt_ref[:, pl.ds(r0, tblk)].astype(F32).T
        for c in range(nchunk):
            lo = c * CHUNK
            q_c = q_ref[pl.ds(r0 + lo, CHUNK), :].astype(F32)
            a_c = a_s[slot, lo:lo + CHUNK, :]

            def pair(s, acc):
                w = q_c * k_s[slot, pl.ds(lo + s, 1), :] * jnp.exp(
                    jnp.minimum(a_c - a_s[slot, pl.ds(lo + s, 1), :], 0.0))
                r = jnp.sum(w, axis=1, keepdims=True)
                m = (row_c <= s) if rev else (row_c >= s)
                return acc + jnp.where(m, r, 0.0) * v_s[pl.ds(lo + s, 1), :]

            intra_s[slot, lo:lo + CHUNK, :] = lax.fori_loop(
                0, CHUNK, pair, jnp.zeros((CHUNK, LANES), F32))

    def emit(r0, o):
        var = jnp.mean(o * o, axis=-1, keepdims=True)
        gate = g_ref[pl.ds(r0, tblk), :].astype(F32)
        o = o * lax.rsqrt(var + EPS) * ow_ref[...] * (gate * _sigmoid(gate))
        o_ref[pl.ds(r0, tblk), :] = o.astype(o_ref.dtype)

    def sweep(second_half):
        def body(i, carry):
            st = list(carry)
            work = []
            for u in range(HGRN_GROUP):
                blk = i * HGRN_GROUP + u
                for direction in (0, 1):
                    r0 = pl.multiple_of((blk if direction == 0 else nblk - 1 - blk) * tblk, tblk)
                    slot = 2 * u + direction
                    inter, overflow, st[direction] = fast_block(direction, slot, r0, st[direction])
                    work.append((direction, slot, r0, inter, overflow))

            for direction, slot, r0, _, overflow in work:
                @pl.when(overflow)
                def _(direction=direction, slot=slot, r0=r0):
                    exact_intra(direction, slot, r0)

            for _, slot, r0, inter, _ in work:
                o = inter + intra_s[slot]
                if second_half:
                    emit(r0, o + oacc_s[pl.ds(r0, tblk), :])
                else:
                    oacc_s[pl.ds(r0, tblk), :] = o
            return tuple(st)
        return body

    zero = jnp.zeros((LANES, LANES), F32)
    steps = nblk // HGRN_GROUP
    carry = lax.fori_loop(0, steps // 2, sweep(False), (zero, zero))
    lax.fori_loop(steps // 2, steps, sweep(True), carry)


def _hgrn(pb, pvt, pz, lb_logits, onorm_w, *, layer, batch, seq, tblk=256):
    depth = lb_logits.shape[0] // 2
    n = batch * seq
    tblk = min(tblk, seq)

    def slab(base):
        return pl.BlockSpec((None, seq, LANES), lambda b, h: (base + h, b, 0))

    assert seq % (2 * HGRN_GROUP * tblk) == 0
    blocks = 4 * _nbytes((seq, LANES), BF16) + 2 * _nbytes((seq, LANES), F32)
    scratch = _nbytes((seq, LANES), F32) + (1 + 6 * HGRN_GROUP) * _nbytes((tblk, LANES), F32)
    return pl.pallas_call(
        functools.partial(_hgrn_kernel, layer=layer, depth=depth, tblk=tblk),
        grid=(batch, HEADS),
        in_specs=[
            slab(0),
            pl.BlockSpec((None, LANES, seq), lambda b, h: (h, 0, b)),
            slab(HEADS),
            pl.BlockSpec((None, seq, LANES), lambda b, h: (h, b, 0)),
            pl.BlockSpec((None, seq, LANES), lambda b, h: (HEADS + h, b, 0)),
            pl.BlockSpec((2 * depth, LANES), lambda b, h: (0, h)),
            pl.BlockSpec((1, LANES), lambda b, h: (0, 0)),
        ],
        out_specs=pl.BlockSpec((None, seq, LANES), lambda b, h: (h, b, 0)),
        out_shape=jax.ShapeDtypeStruct((HEADS, n, LANES), BF16),
        scratch_shapes=[
            pltpu.VMEM((seq, LANES), F32),
            pltpu.VMEM((2 * HGRN_GROUP, tblk, LANES), F32),
            pltpu.VMEM((2 * HGRN_GROUP, tblk, LANES), F32),
            pltpu.VMEM((tblk, LANES), F32),
            pltpu.VMEM((2 * HGRN_GROUP, tblk, LANES), F32),
        ],
        compiler_params=pltpu.CompilerParams(
            dimension_semantics=("parallel", "parallel"),
            vmem_limit_bytes=_vmem_limit(blocks, scratch)),
        name="hgrn",
    )(pb, pvt, pb, pz, pz, lb_logits, onorm_w.reshape(1, LANES))


HALF = LANES // 2
LANE_SHIFT = HALF
LANE_THI = HALF + 1
LANE_TLO = HALF + 2
LANE_SHI = HALF + 3
LANE_SLO = HALF + 4
POS_SPLIT = 256
ROWSUM_MIN = 1e-24
ROWSUM_MAX = 1e30
ATTN_GROUP = 2
ONES_ROWS = 16


def _half_rms_scale(x, lo_mask):
    sq = x * x
    half = x.shape[1] // 2
    ss_lo = jnp.sum(jnp.where(lo_mask, sq, 0.0), axis=1, keepdims=True)
    ss_hi = jnp.sum(jnp.where(lo_mask, 0.0, sq), axis=1, keepdims=True)
    return jnp.where(lo_mask, lax.rsqrt(ss_lo / half + EPS), lax.rsqrt(ss_hi / half + EPS))


def _split_pos(pos):
    return (pos & ~(POS_SPLIT - 1)).astype(F32), (pos & (POS_SPLIT - 1)).astype(F32)


def _attn_kernel(q_ref, k_ref, vt_ref, qw_ref, kw_ref, lam_ref, swc_ref, o_ref,
                 k1_s, k2_s, va_s, kpos_s, qpos_s, dist_s, q_s, bias_s, acc_s, m_s, l_s, accx_s,
                 *, tk, lam_init):
    tq = q_ref.shape[0] // ATTN_GROUP
    seq = k_ref.shape[0]
    nk = seq // tk
    nvar = tk // tq
    h = pl.program_id(1)
    qi = pl.program_id(2)
    lane = lax.broadcasted_iota(jnp.int32, (1, LANES), 1)
    lo_mask = lane < HALF
    slope = lax.bitcast_convert_type(jnp.full((1, LANES), (126 - h) << 23, jnp.int32), F32)

    @pl.when((pl.program_id(0) == 0) & (h == 0) & (qi == 0))
    def _():
        hi, lo = _split_pos(lax.broadcasted_iota(jnp.int32, (seq, LANES), 0))
        kpos_s[...] = jnp.where(lane == LANE_SHI, hi, jnp.where(
            lane == LANE_SLO, lo, jnp.where((lane >= LANE_SHIFT) & (lane <= LANE_TLO), 1.0, 0.0)))
        qpos_s[...] = jnp.where(lane == LANE_THI, -hi, jnp.where(
            lane == LANE_TLO, -lo, jnp.where((lane == LANE_SHI) | (lane == LANE_SLO), 1.0, 0.0)))
        rel = (lax.broadcasted_iota(jnp.int32, (tk, tq), 1) - lax.broadcasted_iota(jnp.int32, (tk, tq), 0))
        for v in range(nvar):
            dist_s[v] = jnp.abs(rel + v * tq).astype(F32)

    @pl.when(qi == 0)
    def _():
        for c in range(nk):
            rows = slice(c * tk, (c + 1) * tk)
            kf = k_ref[rows, :].astype(F32)
            kn = kf * _half_rms_scale(kf, lo_mask) * kw_ref[...]
            pos = kpos_s[rows, :]
            k1_s[rows, :] = jnp.where(lo_mask, kn, pos).astype(BF16)
            k2_s[rows, :] = jnp.where(lo_mask, pltpu.roll(kn, HALF, 1), pos).astype(BF16)
        va_s[:LANES, :] = vt_ref[...]
        va_s[LANES:, :] = jnp.ones((ONES_ROWS, seq), BF16)
        for v in range(nvar):
            bias_s[v] = dist_s[v] * (-slope[:, :1])

    def gain_max(w_ref, mask):
        return jnp.max(jnp.where(mask, jnp.abs(w_ref[...]), 0.0), axis=1, keepdims=True)

    m1 = math.sqrt(HALF) * gain_max(qw_ref, lo_mask) * gain_max(kw_ref, lo_mask)
    m2 = math.sqrt(HALF) * gain_max(qw_ref, ~lo_mask) * gain_max(kw_ref, ~lo_mask)

    lam_p = lam_ref[...]
    lam = (jnp.exp(jnp.sum(lam_p[0:1] * lam_p[1:2], axis=1, keepdims=True))
           - jnp.exp(jnp.sum(lam_p[2:3] * lam_p[3:4], axis=1, keepdims=True)) + lam_init)

    work = []
    for u in range(ATTN_GROUP):
        qb = qi * ATTN_GROUP + u
        qf = q_ref[u * tq:(u + 1) * tq, :].astype(F32)
        qn = qf * _half_rms_scale(qf, lo_mask) * qw_ref[...] * (HALF ** -0.5)
        q1 = jnp.where(lo_mask, qn, 0.0)
        q2 = jnp.where(lo_mask, pltpu.roll(qn, HALF, 1), 0.0)
        q1m = jnp.where(lane == LANE_SHIFT, -m1, q1)
        q2m = jnp.where(lane == LANE_SHIFT, -m2, q2)
        pen = qpos_s[pl.ds(pl.multiple_of(qb * tq, tq), tq), :] * slope
        for v, sigma in enumerate((1.0, -1.0, 0.0)):
            q_s[6 * u + v] = (q1m + sigma * pen).astype(BF16)
            q_s[6 * u + 3 + v] = (q2m + sigma * pen).astype(BF16)
        work.append((u, qb, q1, q2))

    for u, qb, _, _ in work:
        kjm = (qb * tq) // tk
        diag_tile = (qb * tq - kjm * tk) // tq
        for r in range(nk):
            if r == 0:
                kj, ver = kjm, 2
            else:
                kj = kjm + r
                kj = jnp.where(kj >= nk, kj - nk, kj)
                ver = jnp.where(kj < kjm, 0, 1)
            k0 = pl.multiple_of(kj * tk, tk)
            s1 = _dot_nt(k1_s[pl.ds(k0, tk), :], q_s[6 * u + ver])
            s2 = _dot_nt(k2_s[pl.ds(k0, tk), :], q_s[6 * u + 3 + ver])
            if r == 0:
                bias = bias_s[diag_tile]
                s1, s2 = s1 + bias, s2 + bias
            vv = va_s[:, pl.ds(k0, tk)]
            p1 = jnp.exp(s1).astype(BF16)
            p2 = jnp.exp(s2).astype(BF16)
            if r == 0:
                acc_s[2 * u] = _dot(vv, p1)
                acc_s[2 * u + 1] = _dot(vv, p2)
            else:
                acc_s[2 * u] += _dot(vv, p1)
                acc_s[2 * u + 1] += _dot(vv, p2)

    def finish(u, o1t, o2t):
        ot = o1t - lam * o2t
        var = jnp.mean(ot * ot, axis=0, keepdims=True)
        ot = ot * lax.rsqrt(var + EPS) * swc_ref[...] * (1.0 - lam_init)
        o_ref[u * tq:(u + 1) * tq, :] = ot.T.astype(o_ref.dtype)

    unsafe = []
    for u, _, _, _ in work:
        a1 = acc_s[2 * u]
        a2 = acc_s[2 * u + 1]
        l1 = a1[LANES:LANES + 1, :]
        l2 = a2[LANES:LANES + 1, :]
        finish(u, a1[:LANES] / l1, a2[:LANES] / l2)
        lmin = jnp.minimum(jnp.min(l1), jnp.min(l2))
        lmax = jnp.maximum(jnp.max(l1), jnp.max(l2))
        unsafe.append(jnp.logical_not((lmin >= ROWSUM_MIN) & (lmax <= ROWSUM_MAX)))

    for (u, qb, q1, q2), redo in zip(work, unsafe):
        @pl.when(redo)
        def _(u=u, qb=qb, q1=q1, q2=q2):
            q_both = jnp.concatenate([q1, q2], axis=0).astype(BF16)
            m_s[...] = jnp.full(m_s.shape, -jnp.inf, F32)
            l_s[...] = jnp.zeros(l_s.shape, F32)
            accx_s[...] = jnp.zeros(accx_s.shape, F32)
            rel = (lax.broadcasted_iota(jnp.int32, (tq, tk), 0) - lax.broadcasted_iota(jnp.int32, (tq, tk), 1)
                   + qb * tq).astype(F32)

            def exact_body(kj, carry):
                k0 = pl.multiple_of(kj * tk, tk)
                s = jnp.concatenate([_dot_nt(q_both[:tq], k1_s[pl.ds(k0, tk), :]),
                                     _dot_nt(q_both[tq:], k2_s[pl.ds(k0, tk), :])], axis=0)
                pen = jnp.abs(rel - lax.convert_element_type(kj * tk, F32)) * (-slope[:, :1])
                s = s + jnp.concatenate([pen, pen], axis=0)
                m_prev = m_s[...]
                m_new = jnp.maximum(m_prev, jnp.max(s, axis=1, keepdims=True))
                alpha = jnp.exp(m_prev - m_new)
                p = jnp.exp(s - m_new)
                l_s[...] = alpha * l_s[...] + jnp.sum(p, axis=1, keepdims=True)
                accx_s[...] = alpha * accx_s[...] + _dot_nt(p.astype(BF16), va_s[:LANES, pl.ds(k0, tk)])
                m_s[...] = m_new
                return carry

            lax.fori_loop(0, nk, exact_body, 0)
            acc = accx_s[...]
            l = l_s[...]
            finish(u, (acc[:tq] / l[:tq]).T, (acc[tq:] / l[tq:]).T)


def _attn(pb, pvt, qn_w, kn_w, lam_p, subln_w, *, layer, batch, seq, tq=512, tk=512):
    n = batch * seq
    tq = min(tq, seq)
    tk = min(tk, seq)
    nq = seq // (ATTN_GROUP * tq)
    assert tk % tq == 0 and seq % tk == 0 and seq % (ATTN_GROUP * tq) == 0 and seq <= POS_SPLIT * POS_SPLIT
    lam_init = 0.8 - 0.6 * math.exp(-0.3 * layer)
    blocks = 2 * _nbytes((ATTN_GROUP * tq, LANES), BF16) + 2 * _nbytes((seq, LANES), BF16)
    scratch = (3 * _nbytes((seq, LANES), BF16) + _nbytes((ONES_ROWS, seq), BF16) + 2 * _nbytes((seq, LANES), F32)
               + 6 * ATTN_GROUP * _nbytes((tq, LANES), BF16)
               + (1 + 2 * (tk // tq)) * _nbytes((tk, tq), F32)
               + 2 * ATTN_GROUP * _nbytes((LANES + ONES_ROWS, tq), F32)
               + 3 * _nbytes((2 * tq, LANES), F32) + 4 * _nbytes((tk, tq), F32))
    return pl.pallas_call(
        functools.partial(_attn_kernel, tk=tk, lam_init=lam_init),
        grid=(batch, HEADS, nq),
        in_specs=[
            pl.BlockSpec((None, ATTN_GROUP * tq, LANES), lambda b, h, i: (2 * HEADS + h, b * nq + i, 0)),
            pl.BlockSpec((None, seq, LANES), lambda b, h, i: (3 * HEADS + h, b, 0)),
            pl.BlockSpec((None, LANES, seq), lambda b, h, i: (HEADS + h, 0, b)),
            pl.BlockSpec((1, LANES), lambda b, h, i: (0, 0)),
            pl.BlockSpec((1, LANES), lambda b, h, i: (0, 0)),
            pl.BlockSpec((4, LANES // 2), lambda b, h, i: (0, 0)),
            pl.BlockSpec((LANES, 1), lambda b, h, i: (0, 0)),
        ],
        out_specs=pl.BlockSpec((None, ATTN_GROUP * tq, LANES), lambda b, h, i: (h, b * nq + i, 0)),
        out_shape=jax.ShapeDtypeStruct((HEADS, n, LANES), BF16),
        scratch_shapes=[
            pltpu.VMEM((seq, LANES), BF16),
            pltpu.VMEM((seq, LANES), BF16),
            pltpu.VMEM((LANES + ONES_ROWS, seq), BF16),
            pltpu.VMEM((seq, LANES), F32),
            pltpu.VMEM((seq, LANES), F32),
            pltpu.VMEM((tk // tq, tk, tq), F32),
            pltpu.VMEM((6 * ATTN_GROUP, tq, LANES), BF16),
            pltpu.VMEM((tk // tq, tk, tq), F32),
            pltpu.VMEM((2 * ATTN_GROUP, LANES + ONES_ROWS, tq), F32),
            pltpu.VMEM((2 * tq, 1), F32),
            pltpu.VMEM((2 * tq, 1), F32),
            pltpu.VMEM((2 * tq, LANES), F32),
        ],
        compiler_params=pltpu.CompilerParams(
            dimension_semantics=("arbitrary", "arbitrary", "arbitrary"),
            vmem_limit_bytes=_vmem_limit(blocks, scratch)),
        name="diffattn",
    )(pb, pb, pvt, jnp.tile(qn_w, 2).reshape(1, LANES), jnp.tile(kn_w, 2).reshape(1, LANES),
      lam_p, subln_w.reshape(LANES, 1))


def _outproj_kernel(ohg_ref, oda_ref, x_ref, w_ref, o_ref, mix_s):
    for c in range(HEADS):
        mix_s[:, c * LANES:(c + 1) * LANES] = ohg_ref[c]
        mix_s[:, (HEADS + c) * LANES:(HEADS + c + 1) * LANES] = oda_ref[c]
    o_ref[...] = x_ref[...] + _dot(mix_s[...], w_ref[...])


def _outproj(ohg, oda, x, w, layer, *, tm=512):
    n, d = x.shape
    width = w.shape[1]
    tm = min(tm, n)
    blocks = (2 * _nbytes((HEADS, tm, LANES), BF16) + 2 * _nbytes((tm, d), F32) + _nbytes((width, d), BF16))
    return pl.pallas_call(
        _outproj_kernel,
        grid=(n // tm,),
        in_specs=[
            pl.BlockSpec((HEADS, tm, LANES), lambda i: (0, i, 0)),
            pl.BlockSpec((HEADS, tm, LANES), lambda i: (0, i, 0)),
            pl.BlockSpec((tm, d), lambda i: (i, 0)),
            pl.BlockSpec((None, width, d), lambda i: (layer, 0, 0)),
        ],
        out_specs=pl.BlockSpec((tm, d), lambda i: (i, 0)),
        out_shape=jax.ShapeDtypeStruct((n, d), F32),
        scratch_shapes=[pltpu.VMEM((tm, width), BF16)],
        compiler_params=pltpu.CompilerParams(
            dimension_semantics=("parallel",),
            vmem_limit_bytes=_vmem_limit(blocks, _nbytes((tm, width), BF16))),
        name="outproj",
    )(ohg, oda, x, w)


def _ffn_kernel(x_ref, nw_ref, wg_ref, wu_ref, wd_ref, o_ref, h_s):
    @pl.when(pl.program_id(1) == 0)
    def _():
        x = x_ref[...]
        var = jnp.mean(x * x, axis=-1, keepdims=True)
        h_s[...] = (x * lax.rsqrt(var + EPS) * nw_ref[...]).astype(BF16)
        o_ref[...] = x

    h = h_s[...]
    g = _dot(h, wg_ref[...])
    u = _dot(h, wu_ref[...])
    o_ref[...] += _dot((g * _sigmoid(g) * u).astype(BF16), wd_ref[...])


def _ffn(x, nw, wg, wu, wd, layer, *, tm=1024, tf=512):
    n, d = x.shape
    ff = wg.shape[2]
    tm = min(tm, n)
    blocks = (2 * _nbytes((tm, d), F32) + 3 * _nbytes((d, tf), BF16))
    scratch = _nbytes((tm, d), BF16) + 3 * _nbytes((tm, tf), F32)
    return pl.pallas_call(
        _ffn_kernel,
        grid=(n // tm, ff // tf),
        in_specs=[
            pl.BlockSpec((tm, d), lambda i, f: (jnp.minimum(i + jnp.minimum(f, 1), n // tm - 1), 0)),
            pl.BlockSpec((1, d), lambda i, f: (0, 0)),
            pl.BlockSpec((None, d, tf), lambda i, f: (layer, 0, f)),
            pl.BlockSpec((None, d, tf), lambda i, f: (layer, 0, f)),
            pl.BlockSpec((None, tf, d), lambda i, f: (layer, f, 0)),
        ],
        out_specs=pl.BlockSpec((tm, d), lambda i, f: (i, 0)),
        out_shape=jax.ShapeDtypeStruct((n, d), F32),
        scratch_shapes=[pltpu.VMEM((tm, d), BF16)],
        compiler_params=pltpu.CompilerParams(
            dimension_semantics=("parallel", "arbitrary"),
            vmem_limit_bytes=_vmem_limit(blocks, scratch)),
        name="ffn",
    )(x, nw.reshape(1, d), wg, wu, wd)


def kernel(x, norm_mix_w, w_in, hg_lb_logits, hg_onorm_w, da_qnorm_w, da_knorm_w, da_lambda,
           da_subln_w, w_out, norm_ffn_w, w_gate, w_up, w_down):
    batch, seq, d = x.shape
    depth = w_in.shape[0]
    hg = d // 2
    assert hg == HEADS * LANES and seq % CHUNK == 0
    xf = x.reshape(batch * seq, d).astype(F32)
    lb_logits = hg_lb_logits.astype(F32).reshape(2 * depth, hg)

    tn = hg
    w_in_bf = w_in.astype(BF16)
    per_tile = tn // 512
    w_in_t = _transposed_columns(w_in, tuple(3 * per_tile + c for c in range(per_tile))
                                 + tuple(7 * per_tile + c for c in range(per_tile)))
    w_out_bf, w_gate_bf, w_up_bf, w_down_bf = (w.astype(BF16) for w in (w_out, w_gate, w_up, w_down))

    for l in range(depth):
        pb, pz, pvt = _proj(xf, norm_mix_w[l], w_in_bf, w_in_t, l, (0, 4, 5, 6), (1, 2), tn=tn)
        ohg = _hgrn(pb, pvt, pz, lb_logits, hg_onorm_w[l], layer=l, batch=batch, seq=seq)
        oda = _attn(pb, pvt, da_qnorm_w[l], da_knorm_w[l], da_lambda[l].astype(F32), da_subln_w[l],
                    layer=l, batch=batch, seq=seq)
        x1 = _outproj(ohg, oda, xf, w_out_bf, l)
        xf = _ffn(x1, norm_ffn_w[l], w_gate_bf, w_up_bf, w_down_bf, l)
    return xf.reshape(batch, seq, d).astype(x.dtype)
```

```python
import functools
import math

import jax
import jax.numpy as jnp
from jax import lax
from jax.experimental import pallas as pl
from jax.experimental.pallas import tpu as pltpu

F32 = jnp.float32
BF16 = jnp.bfloat16

LANES = 128
HEADS = 8
EPS = 1e-6
LB_FLOOR = 1e-30
LB_CEIL = 1.0 - 1e-6
CHUNK = 64
HGRN_GROUP = 2
EXP_CLAMP = 80.0
VMEM_CAP = 60 * 1024 * 1024
VMEM_SLACK = 12 * 1024 * 1024


def _vmem_limit(pipelined_bytes, scratch_bytes=0):
    return int(min(VMEM_CAP, 2 * pipelined_bytes + scratch_bytes + VMEM_SLACK))


def _nbytes(shape, dtype):
    return math.prod(shape) * jnp.dtype(dtype).itemsize


def _dot(a, b):
    return jnp.dot(a, b, preferred_element_type=F32)


def _dot_nt(a, b):
    return lax.dot_general(a, b, (((1,), (1,)), ((), ())), preferred_element_type=F32)


def _sigmoid(x):
    return 1.0 / (1.0 + jnp.exp(-x))


def _proj_kernel(x_ref, nw_ref, w_ref, wt_ref, ob_ref, oz_ref, ot_ref, h_s, *, nb, nz):
    j = pl.program_id(1)

    @pl.when(j == 0)
    def _():
        x = x_ref[...]
        var = jnp.mean(x * x, axis=-1, keepdims=True)
        h_s[...] = (x * lax.rsqrt(var + EPS) * nw_ref[...]).astype(BF16)

    def slabs(o_ref):
        r = _dot(h_s[...], w_ref[...])
        for c in range(o_ref.shape[0]):
            o_ref[c] = r[:, c * LANES:(c + 1) * LANES].astype(o_ref.dtype)

    @pl.when(j < nb)
    def _():
        slabs(ob_ref)

    @pl.when((j >= nb) & (j < nb + nz))
    def _():
        slabs(oz_ref)

    @pl.when(j >= nb + nz)
    def _():
        r = _dot_nt(wt_ref[...], h_s[...])
        for c in range(ot_ref.shape[0]):
            ot_ref[c] = r[c * LANES:(c + 1) * LANES, :].astype(ot_ref.dtype)


def _pick(j, values):
    out = values[-1]
    for idx in range(len(values) - 2, -1, -1):
        out = jnp.where(j <= idx, values[idx], out)
    return out


def _wt_kernel(w_ref, o_ref):
    o_ref[...] = w_ref[...].T.astype(o_ref.dtype)


WT_TILE = 512


def _transposed_columns(w, col_tiles, *, t=WT_TILE):
    depth, d, _ = w.shape
    return pl.pallas_call(
        _wt_kernel,
        grid=(depth, len(col_tiles), d // t),
        in_specs=[pl.BlockSpec((None, t, t), lambda l, c, r: (l, r, _pick(c, col_tiles)))],
        out_specs=pl.BlockSpec((None, t, t), lambda l, c, r: (l, c, r)),
        out_shape=jax.ShapeDtypeStruct((depth, len(col_tiles) * t, d), BF16),
        compiler_params=pltpu.CompilerParams(
            dimension_semantics=("parallel", "parallel", "parallel"),
            vmem_limit_bytes=_vmem_limit(_nbytes((t, t), F32) + _nbytes((t, t), BF16), 2 * _nbytes((t, t), F32))),
        name="wtranspose",
    )(w)


def _proj(x, nw, w, wt, layer, tiles_b, tiles_z, *, tm=1024, tn=1024):
    n, d = x.shape
    tm = min(tm, n)
    nb, nz, nt = len(tiles_b), len(tiles_z), wt.shape[1] // tn
    assert nt * tn == wt.shape[1] and w.shape[2] % tn == 0
    per = tn // LANES
    main_tiles = tuple(tiles_b) + tuple(tiles_z)
    blocks = (_nbytes((tm, d), F32) + 2 * _nbytes((d, tn), BF16) + _nbytes((tm, tn), BF16)
              + _nbytes((tm, tn), F32) + _nbytes((tm, tn), BF16))
    return pl.pallas_call(
        functools.partial(_proj_kernel, nb=nb, nz=nz),
        grid=(n // tm, nb + nz + nt),
        in_specs=[
            pl.BlockSpec((tm, d), lambda i, j: (jnp.minimum(i + jnp.minimum(j, 1), n // tm - 1), 0)),
            pl.BlockSpec((1, d), lambda i, j: (0, 0)),
            pl.BlockSpec((None, d, tn), lambda i, j: (layer, 0, _pick(j, main_tiles))),
            pl.BlockSpec((None, tn, d), lambda i, j: (layer, jnp.clip(j - nb - nz, 0, nt - 1), 0)),
        ],
        out_specs=[
            pl.BlockSpec((per, tm, LANES), lambda i, j: (jnp.minimum(j, nb - 1), i, 0)),
            pl.BlockSpec((per, tm, LANES), lambda i, j: (jnp.clip(j - nb, 0, nz - 1), i, 0)),
            pl.BlockSpec((per, LANES, tm), lambda i, j: (jnp.clip(j - nb - nz, 0, nt - 1), 0, i)),
        ],
        out_shape=[
            jax.ShapeDtypeStruct((nb * per, n, LANES), BF16),
            jax.ShapeDtypeStruct((nz * per, n, LANES), F32),
            jax.ShapeDtypeStruct((nt * per, LANES, n), BF16),
        ],
        scratch_shapes=[pltpu.VMEM((tm, d), BF16)],
        compiler_params=pltpu.CompilerParams(
            dimension_semantics=("parallel", "arbitrary"),
            vmem_limit_bytes=_vmem_limit(blocks, _nbytes((tm, d), BF16) + _nbytes((tm, tn), F32))),
        name="proj",
    )(x, nw.reshape(1, d), w, wt)


def _seg_cumsum(x, seg, rev):
    n = x.shape[0]
    pos = lax.broadcasted_iota(jnp.int32, x.shape, 0) & (seg - 1)
    s = 1
    while s < seg:
        if rev:
            x = x + jnp.where(pos < seg - s, pltpu.roll(x, n - s, 0), 0.0)
        else:
            x = x + jnp.where(pos >= s, pltpu.roll(x, s, 0), 0.0)
        s *= 2
    return x


def _rows_bcast(x, rows, reps):
    return jnp.concatenate([jnp.broadcast_to(x[r:r + 1, :], (reps, x.shape[1])) for r in rows], axis=0)


def _hgrn_kernel(q_ref, vt_ref, g_ref, zf_ref, zb_ref, lbl_ref, ow_ref, o_ref,
                 oacc_s, k_s, a_s, v_s, intra_s, *, layer, depth, tblk):
    seq = q_ref.shape[0]
    nblk = seq // tblk
    nchunk = tblk // CHUNK

    def lower_bound(direction):
        lg = lbl_ref[direction * depth:(direction + 1) * depth, :]
        e = jnp.exp(lg - jnp.max(lg, axis=0, keepdims=True))
        p = e / jnp.sum(e, axis=0, keepdims=True)
        lb = jnp.sum(p[:layer + 1], axis=0, keepdims=True) - p[0:1]
        lb = jnp.clip(lb, 0.0, LB_CEIL)
        return lb, jnp.maximum(lb, LB_FLOOR)

    row_c = lax.broadcasted_iota(jnp.int32, (CHUNK, LANES), 0)
    sr = lax.broadcasted_iota(jnp.int32, (tblk, tblk), 0)
    sc_ = lax.broadcasted_iota(jnp.int32, (tblk, tblk), 1)
    same_chunk = (sr // CHUNK) == (sc_ // CHUNK)
    own_lanes = (lax.broadcasted_iota(jnp.int32, (tblk, nchunk * LANES), 0) // CHUNK
                 == lax.broadcasted_iota(jnp.int32, (tblk, nchunk * LANES), 1) // LANES)

    bounds = (lower_bound(0), lower_bound(1))
    z_refs = (zf_ref, zb_ref)

    def fast_block(direction, slot, r0, st):
        rev = direction == 1
        lb, lbp = bounds[direction]
        causal = same_chunk & ((sc_ >= sr) if rev else (sc_ <= sr))
        q = q_ref[pl.ds(r0, tblk), :].astype(F32)
        vt = vt_ref[:, pl.ds(r0, tblk)]
        z = z_refs[direction][pl.ds(r0, tblk), :]
        sg = _sigmoid(z)
        lf = jnp.log(lbp + (1.0 - lb) * sg)
        k = (1.0 - lb) * (1.0 - sg) - (lbp - lb)
        a = _seg_cumsum(lf, CHUNK, rev)
        edge = (lambda c: c * CHUNK) if rev else (lambda c: c * CHUNK + CHUNK - 1)
        a_edge = _rows_bcast(a, [edge(c) for c in range(nchunk)], CHUNK)
        d = a - _rows_bcast(a, [c * CHUNK + CHUNK // 2 for c in range(nchunk)], CHUNK)
        overflow = jnp.max(jnp.abs(d)) > EXP_CLAMP
        e_mid = jnp.exp(jnp.clip(d, -EXP_CLAMP, EXP_CLAMP))
        qs = (q * e_mid).astype(BF16)
        ks = (k / e_mid).astype(BF16)
        qa = (q * jnp.exp(a)).astype(BF16)
        kb = (k * jnp.exp(a_edge - a)).astype(BF16)
        k_s[slot] = k
        a_s[slot] = a

        scores = jnp.where(causal, _dot_nt(qs, ks), 0.0).astype(BF16)
        intra_s[slot] = _dot_nt(scores, vt)
        upd = _dot(vt, jnp.where(own_lanes, jnp.concatenate([kb] * nchunk, axis=1), 0.0))
        inter = [None] * nchunk
        for c in (range(nchunk - 1, -1, -1) if rev else range(nchunk)):
            lo = c * CHUNK
            inter[c] = _dot_nt(qa[lo:lo + CHUNK], st.astype(BF16))
            st = st * jnp.exp(a[edge(c):edge(c) + 1, :]) + upd[:, c * LANES:(c + 1) * LANES]
        return jnp.concatenate(inter, axis=0), overflow, st

    def exact_intra(direction, slot, r0):
        rev = direction == 1
        v_s[...] = vt_ref[:, pl.ds(r0, tblk)].astype(F32).T
        for c in range(nchunk):
            lo = c * CHUNK
            q_c = q_ref[pl.ds(r0 + lo, CHUNK), :].astype(F32)
            a_c = a_s[slot, lo:lo + CHUNK, :]

            def pair(s, acc):
                w = q_c * k_s[slot, pl.ds(lo + s, 1), :] * jnp.exp(
                    jnp.minimum(a_c - a_s[slot, pl.ds(lo + s, 1), :], 0.0))
                r = jnp.sum(w, axis=1, keepdims=True)
                m = (row_c <= s) if rev else (row_c >= s)
                return acc + jnp.where(m, r, 0.0) * v_s[pl.ds(lo + s, 1), :]

            intra_s[slot, lo:lo + CHUNK, :] = lax.fori_loop(
                0, CHUNK, pair, jnp.zeros((CHUNK, LANES), F32))

    def emit(r0, o):
        var = jnp.mean(o * o, axis=-1, keepdims=True)
        gate = g_ref[pl.ds(r0, tblk), :].astype(F32)
        o = o * lax.rsqrt(var + EPS) * ow_ref[...] * (gate * _sigmoid(gate))
        o_ref[pl.ds(r0, tblk), :] = o.astype(o_ref.dtype)

    def sweep(second_half):
        def body(i, carry):
            st = list(carry)
            work = []
            for u in range(HGRN_GROUP):
                blk = i * HGRN_GROUP + u
                for direction in (0, 1):
                    r0 = pl.multiple_of((blk if direction == 0 else nblk - 1 - blk) * tblk, tblk)
                    slot = 2 * u + direction
                    inter, overflow, st[direction] = fast_block(direction, slot, r0, st[direction])
                    work.append((direction, slot, r0, inter, overflow))

            for direction, slot, r0, _, overflow in work:
                @pl.when(overflow)
                def _(direction=direction, slot=slot, r0=r0):
                    exact_intra(direction, slot, r0)

            for _, slot, r0, inter, _ in work:
                o = inter + intra_s[slot]
                if second_half:
                    emit(r0, o + oacc_s[pl.ds(r0, tblk), :])
                else:
                    oacc_s[pl.ds(r0, tblk), :] = o
            return tuple(st)
        return body

    zero = jnp.zeros((LANES, LANES), F32)
    steps = nblk // HGRN_GROUP
    carry = lax.fori_loop(0, steps // 2, sweep(False), (zero, zero))
    lax.fori_loop(steps // 2, steps, sweep(True), carry)


def _hgrn(pb, pvt, pz, lb_logits, onorm_w, *, layer, batch, seq, tblk=256):
    depth = lb_logits.shape[0] // 2
    n = batch * seq
    tblk = min(tblk, seq)

    def slab(base):
        return pl.BlockSpec((None, seq, LANES), lambda b, h: (base + h, b, 0))

    assert seq % (2 * HGRN_GROUP * tblk) == 0
    blocks = 4 * _nbytes((seq, LANES), BF16) + 2 * _nbytes((seq, LANES), F32)
    scratch = _nbytes((seq, LANES), F32) + (1 + 6 * HGRN_GROUP) * _nbytes((tblk, LANES), F32)
    return pl.pallas_call(
        functools.partial(_hgrn_kernel, layer=layer, depth=depth, tblk=tblk),
        grid=(batch, HEADS),
        in_specs=[
            slab(0),
            pl.BlockSpec((None, LANES, seq), lambda b, h: (h, 0, b)),
            slab(HEADS),
            pl.BlockSpec((None, seq, LANES), lambda b, h: (h, b, 0)),
            pl.BlockSpec((None, seq, LANES), lambda b, h: (HEADS + h, b, 0)),
            pl.BlockSpec((2 * depth, LANES), lambda b, h: (0, h)),
            pl.BlockSpec((1, LANES), lambda b, h: (0, 0)),
        ],
        out_specs=pl.BlockSpec((None, seq, LANES), lambda b, h: (h, b, 0)),
        out_shape=jax.ShapeDtypeStruct((HEADS, n, LANES), BF16),
        scratch_shapes=[
            pltpu.VMEM((seq, LANES), F32),
            pltpu.VMEM((2 * HGRN_GROUP, tblk, LANES), F32),
            pltpu.VMEM((2 * HGRN_GROUP, tblk, LANES), F32),
            pltpu.VMEM((tblk, LANES), F32),
            pltpu.VMEM((2 * HGRN_GROUP, tblk, LANES), F32),
        ],
        compiler_params=pltpu.CompilerParams(
            dimension_semantics=("parallel", "parallel"),
            vmem_limit_bytes=_vmem_limit(blocks, scratch)),
        name="hgrn",
    )(pb, pvt, pb, pz, pz, lb_logits, onorm_w.reshape(1, LANES))


HALF = LANES // 2
LANE_SHIFT = HALF
LANE_THI = HALF + 1
LANE_TLO = HALF + 2
LANE_SHI = HALF + 3
LANE_SLO = HALF + 4
POS_SPLIT = 256
ROWSUM_MIN = 1e-24
ROWSUM_MAX = 1e30
ATTN_GROUP = 2
ONES_ROWS = 16


def _half_rms_scale(x):
    sq = x * x
    hi = sq.astype(BF16)
    lo = (sq - hi.astype(F32)).astype(BF16)
    r = lax.broadcasted_iota(jnp.int32, (LANES, LANES), 0) // HALF
    c = lax.broadcasted_iota(jnp.int32, (LANES, LANES), 1) // HALF
    ones = jnp.where(r == c, 1.0, 0.0).astype(BF16)
    return lax.rsqrt((_dot(hi, ones) + _dot(lo, ones)) / HALF + EPS)


def _split_pos(pos):
    return (pos & ~(POS_SPLIT - 1)).astype(F32), (pos & (POS_SPLIT - 1)).astype(F32)


def _attn_kernel(q_ref, k_ref, vt_ref, qw_ref, kw_ref, lam_ref, swc_ref, o_ref,
                 k1_s, k2_s, va_s, kpos_s, qpos_s, dist_s, q_s, bias_s, acc_s, m_s, l_s, accx_s,
                 *, tk, lam_init):
    tq = q_ref.shape[0] // ATTN_GROUP
    seq = k_ref.shape[0]
    nk = seq // tk
    nvar = tk // tq
    h = pl.program_id(1)
    qi = pl.program_id(2)
    lane = lax.broadcasted_iota(jnp.int32, (1, LANES), 1)
    lo_mask = lane < HALF
    slope = lax.bitcast_convert_type(jnp.full((1, LANES), (126 - h) << 23, jnp.int32), F32)

    @pl.when((pl.program_id(0) == 0) & (h == 0) & (qi == 0))
    def _():
        hi, lo = _split_pos(lax.broadcasted_iota(jnp.int32, (seq, LANES), 0))
        kpos_s[...] = jnp.where(lane == LANE_SHI, hi, jnp.where(
            lane == LANE_SLO, lo, jnp.where((lane >= LANE_SHIFT) & (lane <= LANE_TLO), 1.0, 0.0)))
        qpos_s[...] = jnp.where(lane == LANE_THI, -hi, jnp.where(
            lane == LANE_TLO, -lo, jnp.where((lane == LANE_SHI) | (lane == LANE_SLO), 1.0, 0.0)))
        rel = (lax.broadcasted_iota(jnp.int32, (tk, tq), 1) - lax.broadcasted_iota(jnp.int32, (tk, tq), 0))
        for v in range(nvar):
            dist_s[v] = jnp.abs(rel + v * tq).astype(F32)

    @pl.when(qi == 0)
    def _():
        for c in range(nk):
            rows = slice(c * tk, (c + 1) * tk)
            kf = k_ref[rows, :].astype(F32)
            kn = kf * _half_rms_scale(kf) * kw_ref[...]
            pos = kpos_s[rows, :]
            k1_s[rows, :] = jnp.where(lo_mask, kn, pos).astype(BF16)
            k2_s[rows, :] = jnp.where(lo_mask, pltpu.roll(kn, HALF, 1), pos).astype(BF16)
        va_s[:LANES, :] = vt_ref[...]
        va_s[LANES:, :] = jnp.ones((ONES_ROWS, seq), BF16)
        for v in range(nvar):
            bias_s[v] = dist_s[v] * (-slope[:, :1])

    def gain_max(w_ref, mask):
        return jnp.max(jnp.where(mask, jnp.abs(w_ref[...]), 0.0), axis=1, keepdims=True)

    m1 = math.sqrt(HALF) * gain_max(qw_ref, lo_mask) * gain_max(kw_ref, lo_mask)
    m2 = math.sqrt(HALF) * gain_max(qw_ref, ~lo_mask) * gain_max(kw_ref, ~lo_mask)

    lam_p = lam_ref[...]
    lam = (jnp.exp(jnp.sum(lam_p[0:1] * lam_p[1:2], axis=1, keepdims=True))
           - jnp.exp(jnp.sum(lam_p[2:3] * lam_p[3:4], axis=1, keepdims=True)) + lam_init)

    work = []
    for u in range(ATTN_GROUP):
        qb = qi * ATTN_GROUP + u
        qf = q_ref[u * tq:(u + 1) * tq, :].astype(F32)
        qn = qf * _half_rms_scale(qf) * qw_ref[...] * (HALF ** -0.5)
        q1 = jnp.where(lo_mask, qn, 0.0)
        q2 = jnp.where(lo_mask, pltpu.roll(qn, HALF, 1), 0.0)
        q1m = jnp.where(lane == LANE_SHIFT, -m1, q1)
        q2m = jnp.where(lane == LANE_SHIFT, -m2, q2)
        pen = qpos_s[pl.ds(pl.multiple_of(qb * tq, tq), tq), :] * slope
        for v, sigma in enumerate((1.0, -1.0, 0.0)):
            q_s[6 * u + v] = (q1m + sigma * pen).astype(BF16)
            q_s[6 * u + 3 + v] = (q2m + sigma * pen).astype(BF16)
        work.append((u, qb, q1, q2))

    for u, qb, _, _ in work:
        kjm = (qb * tq) // tk
        diag_tile = (qb * tq - kjm * tk) // tq
        for r in range(nk):
            if r == 0:
                kj, ver = kjm, 2
            else:
                kj = kjm + r
                kj = jnp.where(kj >= nk, kj - nk, kj)
                ver = jnp.where(kj < kjm, 0, 1)
            k0 = pl.multiple_of(kj * tk, tk)
            s1 = _dot_nt(k1_s[pl.ds(k0, tk), :], q_s[6 * u + ver])
            s2 = _dot_nt(k2_s[pl.ds(k0, tk), :], q_s[6 * u + 3 + ver])
            if r == 0:
                bias = bias_s[diag_tile]
                s1, s2 = s1 + bias, s2 + bias
            vv = va_s[:, pl.ds(k0, tk)]
            p1 = jnp.exp(s1).astype(BF16)
            p2 = jnp.exp(s2).astype(BF16)
            if r == 0:
                acc_s[2 * u] = _dot(vv, p1)
                acc_s[2 * u + 1] = _dot(vv, p2)
            else:
                acc_s[2 * u] += _dot(vv, p1)
                acc_s[2 * u + 1] += _dot(vv, p2)

    def finish(u, o1t, o2t):
        ot = o1t - lam * o2t
        var = jnp.mean(ot * ot, axis=0, keepdims=True)
        ot = ot * lax.rsqrt(var + EPS) * swc_ref[...] * (1.0 - lam_init)
        o_ref[u * tq:(u + 1) * tq, :] = ot.T.astype(o_ref.dtype)

    unsafe = []
    for u, _, _, _ in work:
        a1 = acc_s[2 * u]
        a2 = acc_s[2 * u + 1]
        l1 = a1[LANES:LANES + 1, :]
        l2 = a2[LANES:LANES + 1, :]
        finish(u, a1[:LANES] / l1, a2[:LANES] / l2)
        lmin = jnp.minimum(jnp.min(l1), jnp.min(l2))
        lmax = jnp.maximum(jnp.max(l1), jnp.max(l2))
        unsafe.append(jnp.logical_not((lmin >= ROWSUM_MIN) & (lmax <= ROWSUM_MAX)))

    for (u, qb, q1, q2), redo in zip(work, unsafe):
        @pl.when(redo)
        def _(u=u, qb=qb, q1=q1, q2=q2):
            q_both = jnp.concatenate([q1, q2], axis=0).astype(BF16)
            m_s[...] = jnp.full(m_s.shape, -jnp.inf, F32)
            l_s[...] = jnp.zeros(l_s.shape, F32)
            accx_s[...] = jnp.zeros(accx_s.shape, F32)
            rel = (lax.broadcasted_iota(jnp.int32, (tq, tk), 0) - lax.broadcasted_iota(jnp.int32, (tq, tk), 1)
                   + qb * tq).astype(F32)

            def exact_body(kj, carry):
                k0 = pl.multiple_of(kj * tk, tk)
                s = jnp.concatenate([_dot_nt(q_both[:tq], k1_s[pl.ds(k0, tk), :]),
                                     _dot_nt(q_both[tq:], k2_s[pl.ds(k0, tk), :])], axis=0)
                pen = jnp.abs(rel - lax.convert_element_type(kj * tk, F32)) * (-slope[:, :1])
                s = s + jnp.concatenate([pen, pen], axis=0)
                m_prev = m_s[...]
                m_new = jnp.maximum(m_prev, jnp.max(s, axis=1, keepdims=True))
                alpha = jnp.exp(m_prev - m_new)
                p = jnp.exp(s - m_new)
                l_s[...] = alpha * l_s[...] + jnp.sum(p, axis=1, keepdims=True)
                accx_s[...] = alpha * accx_s[...] + _dot_nt(p.astype(BF16), va_s[:LANES, pl.ds(k0, tk)])
                m_s[...] = m_new
                return carry

            lax.fori_loop(0, nk, exact_body, 0)
            acc = accx_s[...]
            l = l_s[...]
            finish(u, (acc[:tq] / l[:tq]).T, (acc[tq:] / l[tq:]).T)


def _attn(pb, pvt, qn_w, kn_w, lam_p, subln_w, *, layer, batch, seq, tq=512, tk=512):
    n = batch * seq
    tq = min(tq, seq)
    tk = min(tk, seq)
    nq = seq // (ATTN_GROUP * tq)
    assert tk % tq == 0 and seq % tk == 0 and seq % (ATTN_GROUP * tq) == 0 and seq <= POS_SPLIT * POS_SPLIT
    lam_init = 0.8 - 0.6 * math.exp(-0.3 * layer)
    blocks = 2 * _nbytes((ATTN_GROUP * tq, LANES), BF16) + 2 * _nbytes((seq, LANES), BF16)
    scratch = (3 * _nbytes((seq, LANES), BF16) + _nbytes((ONES_ROWS, seq), BF16) + 2 * _nbytes((seq, LANES), F32)
               + 6 * ATTN_GROUP * _nbytes((tq, LANES), BF16)
               + (1 + 2 * (tk // tq)) * _nbytes((tk, tq), F32)
               + 2 * ATTN_GROUP * _nbytes((LANES + ONES_ROWS, tq), F32)
               + 3 * _nbytes((2 * tq, LANES), F32) + 4 * _nbytes((tk, tq), F32))
    return pl.pallas_call(
        functools.partial(_attn_kernel, tk=tk, lam_init=lam_init),
        grid=(batch, HEADS, nq),
        in_specs=[
            pl.BlockSpec((None, ATTN_GROUP * tq, LANES), lambda b, h, i: (2 * HEADS + h, b * nq + i, 0)),
            pl.BlockSpec((None, seq, LANES), lambda b, h, i: (3 * HEADS + h, b, 0)),
            pl.BlockSpec((None, LANES, seq), lambda b, h, i: (HEADS + h, 0, b)),
            pl.BlockSpec((1, LANES), lambda b, h, i: (0, 0)),
            pl.BlockSpec((1, LANES), lambda b, h, i: (0, 0)),
            pl.BlockSpec((4, LANES // 2), lambda b, h, i: (0, 0)),
            pl.BlockSpec((LANES, 1), lambda b, h, i: (0, 0)),
        ],
        out_specs=pl.BlockSpec((None, ATTN_GROUP * tq, LANES), lambda b, h, i: (h, b * nq + i, 0)),
        out_shape=jax.ShapeDtypeStruct((HEADS, n, LANES), BF16),
        scratch_shapes=[
            pltpu.VMEM((seq, LANES), BF16),
            pltpu.VMEM((seq, LANES), BF16),
            pltpu.VMEM((LANES + ONES_ROWS, seq), BF16),
            pltpu.VMEM((seq, LANES), F32),
            pltpu.VMEM((seq, LANES), F32),
            pltpu.VMEM((tk // tq, tk, tq), F32),
            pltpu.VMEM((6 * ATTN_GROUP, tq, LANES), BF16),
            pltpu.VMEM((tk // tq, tk, tq), F32),
            pltpu.VMEM((2 * ATTN_GROUP, LANES + ONES_ROWS, tq), F32),
            pltpu.VMEM((2 * tq, 1), F32),
            pltpu.VMEM((2 * tq, 1), F32),
            pltpu.VMEM((2 * tq, LANES), F32),
        ],
        compiler_params=pltpu.CompilerParams(
            dimension_semantics=("arbitrary", "arbitrary", "arbitrary"),
            vmem_limit_bytes=_vmem_limit(blocks, scratch)),
        name="diffattn",
    )(pb, pb, pvt, jnp.tile(qn_w, 2).reshape(1, LANES), jnp.tile(kn_w, 2).reshape(1, LANES),
      lam_p, subln_w.reshape(LANES, 1))


def _outproj_kernel(ohg_ref, oda_ref, x_ref, w_ref, o_ref, mix_s):
    for c in range(HEADS):
        mix_s[:, c * LANES:(c + 1) * LANES] = ohg_ref[c]
        mix_s[:, (HEADS + c) * LANES:(HEADS + c + 1) * LANES] = oda_ref[c]
    o_ref[...] = x_ref[...] + _dot(mix_s[...], w_ref[...])


def _outproj(ohg, oda, x, w, layer, *, tm=512):
    n, d = x.shape
    width = w.shape[1]
    tm = min(tm, n)
    blocks = (2 * _nbytes((HEADS, tm, LANES), BF16) + 2 * _nbytes((tm, d), F32) + _nbytes((width, d), BF16))
    return pl.pallas_call(
        _outproj_kernel,
        grid=(n // tm,),
        in_specs=[
            pl.BlockSpec((HEADS, tm, LANES), lambda i: (0, i, 0)),
            pl.BlockSpec((HEADS, tm, LANES), lambda i: (0, i, 0)),
            pl.BlockSpec((tm, d), lambda i: (i, 0)),
            pl.BlockSpec((None, width, d), lambda i: (layer, 0, 0)),
        ],
        out_specs=pl.BlockSpec((tm, d), lambda i: (i, 0)),
        out_shape=jax.ShapeDtypeStruct((n, d), F32),
        scratch_shapes=[pltpu.VMEM((tm, width), BF16)],
        compiler_params=pltpu.CompilerParams(
            dimension_semantics=("parallel",),
            vmem_limit_bytes=_vmem_limit(blocks, _nbytes((tm, width), BF16))),
        name="outproj",
    )(ohg, oda, x, w)


def _ffn_kernel(x_ref, nw_ref, wg_ref, wu_ref, wd_ref, o_ref, h_s):
    @pl.when(pl.program_id(1) == 0)
    def _():
        x = x_ref[...]
        var = jnp.mean(x * x, axis=-1, keepdims=True)
        h_s[...] = (x * lax.rsqrt(var + EPS) * nw_ref[...]).astype(BF16)
        o_ref[...] = x

    h = h_s[...]
    g = _dot(h, wg_ref[...])
    u = _dot(h, wu_ref[...])
    o_ref[...] += _dot((g * _sigmoid(g) * u).astype(BF16), wd_ref[...])


def _ffn(x, nw, wg, wu, wd, layer, *, tm=1024, tf=512):
    n, d = x.shape
    ff = wg.shape[2]
    tm = min(tm, n)
    blocks = (2 * _nbytes((tm, d), F32) + 3 * _nbytes((d, tf), BF16))
    scratch = _nbytes((tm, d), BF16) + 3 * _nbytes((tm, tf), F32)
    return pl.pallas_call(
        _ffn_kernel,
        grid=(n // tm, ff // tf),
        in_specs=[
            pl.BlockSpec((tm, d), lambda i, f: (jnp.minimum(i + jnp.minimum(f, 1), n // tm - 1), 0)),
            pl.BlockSpec((1, d), lambda i, f: (0, 0)),
            pl.BlockSpec((None, d, tf), lambda i, f: (layer, 0, f)),
            pl.BlockSpec((None, d, tf), lambda i, f: (layer, 0, f)),
            pl.BlockSpec((None, tf, d), lambda i, f: (layer, f, 0)),
        ],
        out_specs=pl.BlockSpec((tm, d), lambda i, f: (i, 0)),
        out_shape=jax.ShapeDtypeStruct((n, d), F32),
        scratch_shapes=[pltpu.VMEM((tm, d), BF16)],
        compiler_params=pltpu.CompilerParams(
            dimension_semantics=("parallel", "arbitrary"),
            vmem_limit_bytes=_vmem_limit(blocks, scratch)),
        name="ffn",
    )(x, nw.reshape(1, d), wg, wu, wd)


def kernel(x, norm_mix_w, w_in, hg_lb_logits, hg_onorm_w, da_qnorm_w, da_knorm_w, da_lambda,
           da_subln_w, w_out, norm_ffn_w, w_gate, w_up, w_down):
    batch, seq, d = x.shape
    depth = w_in.shape[0]
    hg = d // 2
    assert hg == HEADS * LANES and seq % CHUNK == 0
    xf = x.reshape(batch * seq, d).astype(F32)
    lb_logits = hg_lb_logits.astype(F32).reshape(2 * depth, hg)

    tn = hg
    w_in_bf = w_in.astype(BF16)
    per_tile = tn // WT_TILE
    w_in_t = _transposed_columns(w_in, tuple(3 * per_tile + c for c in range(per_tile))
                                 + tuple(7 * per_tile + c for c in range(per_tile)))
    w_out_bf, w_gate_bf, w_up_bf, w_down_bf = (w.astype(BF16) for w in (w_out, w_gate, w_up, w_down))

    for l in range(depth):
        pb, pz, pvt = _proj(xf, norm_mix_w[l], w_in_bf, w_in_t, l, (0, 4, 5, 6), (1, 2), tn=tn)
        ohg = _hgrn(pb, pvt, pz, lb_logits, hg_onorm_w[l], layer=l, batch=batch, seq=seq)
        oda = _attn(pb, pvt, da_qnorm_w[l], da_knorm_w[l], da_lambda[l].astype(F32), da_subln_w[l],
                    layer=l, batch=batch, seq=seq)
        x1 = _outproj(ohg, oda, xf, w_out_bf, l)
        xf = _ffn(x1, norm_ffn_w[l], w_gate_bf, w_up_bf, w_down_bf, l)
    return xf.reshape(batch, seq, d).astype(x.dtype)
```

```python
import functools
import math

import jax
import jax.numpy as jnp
from jax import lax
from jax.experimental import pallas as pl
from jax.experimental.pallas import tpu as pltpu

F32 = jnp.float32
BF16 = jnp.bfloat16

LANES = 128
HEADS = 8
EPS = 1e-6
LB_FLOOR = 1e-30
LB_CEIL = 1.0 - 1e-6
CHUNK = 64
HGRN_GROUP = 4
EXP_CLAMP = 80.0
VMEM_CAP = 60 * 1024 * 1024
VMEM_SLACK = 12 * 1024 * 1024


def _vmem_limit(pipelined_bytes, scratch_bytes=0):
    return int(min(VMEM_CAP, 2 * pipelined_bytes + scratch_bytes + VMEM_SLACK))


def _nbytes(shape, dtype):
    return math.prod(shape) * jnp.dtype(dtype).itemsize


def _dot(a, b):
    return jnp.dot(a, b, preferred_element_type=F32)


def _dot_nt(a, b):
    return lax.dot_general(a, b, (((1,), (1,)), ((), ())), preferred_element_type=F32)


def _sigmoid(x):
    return 1.0 / (1.0 + jnp.exp(-x))


def _proj_kernel(x_ref, nw_ref, w_ref, wt_ref, ob_ref, oz_ref, ot_ref, h_s, *, nb, nz):
    j = pl.program_id(1)

    @pl.when(j == 0)
    def _():
        x = x_ref[...]
        var = jnp.mean(x * x, axis=-1, keepdims=True)
        h_s[...] = (x * lax.rsqrt(var + EPS) * nw_ref[...]).astype(BF16)

    def slabs(o_ref):
        r = _dot(h_s[...], w_ref[...])
        for c in range(o_ref.shape[0]):
            o_ref[c] = r[:, c * LANES:(c + 1) * LANES].astype(o_ref.dtype)

    @pl.when(j < nb)
    def _():
        slabs(ob_ref)

    @pl.when((j >= nb) & (j < nb + nz))
    def _():
        slabs(oz_ref)

    @pl.when(j >= nb + nz)
    def _():
        r = _dot_nt(wt_ref[...], h_s[...])
        for c in range(ot_ref.shape[0]):
            ot_ref[c] = r[c * LANES:(c + 1) * LANES, :].astype(ot_ref.dtype)


def _pick(j, values):
    out = values[-1]
    for idx in range(len(values) - 2, -1, -1):
        out = jnp.where(j <= idx, values[idx], out)
    return out


def _wt_kernel(w_ref, o_ref):
    o_ref[...] = w_ref[...].T.astype(o_ref.dtype)


WT_TILE = 512


def _transposed_columns(w, col_tiles, *, t=WT_TILE):
    depth, d, _ = w.shape
    return pl.pallas_call(
        _wt_kernel,
        grid=(depth, len(col_tiles), d // t),
        in_specs=[pl.BlockSpec((None, t, t), lambda l, c, r: (l, r, _pick(c, col_tiles)))],
        out_specs=pl.BlockSpec((None, t, t), lambda l, c, r: (l, c, r)),
        out_shape=jax.ShapeDtypeStruct((depth, len(col_tiles) * t, d), BF16),
        compiler_params=pltpu.CompilerParams(
            dimension_semantics=("parallel", "parallel", "parallel"),
            vmem_limit_bytes=_vmem_limit(_nbytes((t, t), F32) + _nbytes((t, t), BF16), 2 * _nbytes((t, t), F32))),
        name="wtranspose",
    )(w)


def _proj(x, nw, w, wt, layer, tiles_b, tiles_z, *, tm=1024, tn=1024):
    n, d = x.shape
    tm = min(tm, n)
    nb, nz, nt = len(tiles_b), len(tiles_z), wt.shape[1] // tn
    assert nt * tn == wt.shape[1] and w.shape[2] % tn == 0
    per = tn // LANES
    main_tiles = tuple(tiles_b) + tuple(tiles_z)
    blocks = (_nbytes((tm, d), F32) + 2 * _nbytes((d, tn), BF16) + _nbytes((tm, tn), BF16)
              + _nbytes((tm, tn), F32) + _nbytes((tm, tn), BF16))
    return pl.pallas_call(
        functools.partial(_proj_kernel, nb=nb, nz=nz),
        grid=(n // tm, nb + nz + nt),
        in_specs=[
            pl.BlockSpec((tm, d), lambda i, j: (jnp.minimum(i + jnp.minimum(j, 1), n // tm - 1), 0)),
            pl.BlockSpec((1, d), lambda i, j: (0, 0)),
            pl.BlockSpec((None, d, tn), lambda i, j: (layer, 0, _pick(j, main_tiles))),
            pl.BlockSpec((None, tn, d), lambda i, j: (layer, jnp.clip(j - nb - nz, 0, nt - 1), 0)),
        ],
        out_specs=[
            pl.BlockSpec((per, tm, LANES), lambda i, j: (jnp.minimum(j, nb - 1), i, 0)),
            pl.BlockSpec((per, tm, LANES), lambda i, j: (jnp.clip(j - nb, 0, nz - 1), i, 0)),
            pl.BlockSpec((per, LANES, tm), lambda i, j: (jnp.clip(j - nb - nz, 0, nt - 1), 0, i)),
        ],
        out_shape=[
            jax.ShapeDtypeStruct((nb * per, n, LANES), BF16),
            jax.ShapeDtypeStruct((nz * per, n, LANES), F32),
            jax.ShapeDtypeStruct((nt * per, LANES, n), BF16),
        ],
        scratch_shapes=[pltpu.VMEM((tm, d), BF16)],
        compiler_params=pltpu.CompilerParams(
            dimension_semantics=("parallel", "arbitrary"),
            vmem_limit_bytes=_vmem_limit(blocks, _nbytes((tm, d), BF16) + _nbytes((tm, tn), F32))),
        name="proj",
    )(x, nw.reshape(1, d), w, wt)


def _seg_cumsum(x, seg, rev):
    n = x.shape[0]
    pos = lax.broadcasted_iota(jnp.int32, x.shape, 0) & (seg - 1)
    s = 1
    while s < seg:
        if rev:
            x = x + jnp.where(pos < seg - s, pltpu.roll(x, n - s, 0), 0.0)
        else:
            x = x + jnp.where(pos >= s, pltpu.roll(x, s, 0), 0.0)
        s *= 2
    return x


def _rows_bcast(x, rows, reps):
    return jnp.concatenate([jnp.broadcast_to(x[r:r + 1, :], (reps, x.shape[1])) for r in rows], axis=0)


def _hgrn_kernel(q_ref, vt_ref, g_ref, zf_ref, zb_ref, lbl_ref, ow_ref, o_ref,
                 oacc_s, k_s, a_s, v_s, intra_s, *, layer, depth, tblk):
    seq = q_ref.shape[0]
    nblk = seq // tblk
    nchunk = tblk // CHUNK

    def lower_bound(direction):
        lg = lbl_ref[direction * depth:(direction + 1) * depth, :]
        e = jnp.exp(lg - jnp.max(lg, axis=0, keepdims=True))
        p = e / jnp.sum(e, axis=0, keepdims=True)
        lb = jnp.sum(p[:layer + 1], axis=0, keepdims=True) - p[0:1]
        lb = jnp.clip(lb, 0.0, LB_CEIL)
        return lb, jnp.maximum(lb, LB_FLOOR)

    row_c = lax.broadcasted_iota(jnp.int32, (CHUNK, LANES), 0)
    sr = lax.broadcasted_iota(jnp.int32, (tblk, tblk), 0)
    sc_ = lax.broadcasted_iota(jnp.int32, (tblk, tblk), 1)
    same_chunk = (sr // CHUNK) == (sc_ // CHUNK)
    own_lanes = (lax.broadcasted_iota(jnp.int32, (tblk, nchunk * LANES), 0) // CHUNK
                 == lax.broadcasted_iota(jnp.int32, (tblk, nchunk * LANES), 1) // LANES)

    bounds = (lower_bound(0), lower_bound(1))
    z_refs = (zf_ref, zb_ref)

    def fast_block(direction, slot, r0, st):
        rev = direction == 1
        lb, lbp = bounds[direction]
        causal = same_chunk & ((sc_ >= sr) if rev else (sc_ <= sr))
        q = q_ref[pl.ds(r0, tblk), :].astype(F32)
        vt = vt_ref[:, pl.ds(r0, tblk)]
        z = z_refs[direction][pl.ds(r0, tblk), :]
        sg = _sigmoid(z)
        lf = jnp.log(lbp + (1.0 - lb) * sg)
        k = (1.0 - lb) * (1.0 - sg) - (lbp - lb)
        a = _seg_cumsum(lf, CHUNK, rev)
        edge = (lambda c: c * CHUNK) if rev else (lambda c: c * CHUNK + CHUNK - 1)
        a_edge = _rows_bcast(a, [edge(c) for c in range(nchunk)], CHUNK)
        d = a - _rows_bcast(a, [c * CHUNK + CHUNK // 2 for c in range(nchunk)], CHUNK)
        overflow = jnp.max(jnp.abs(d)) > EXP_CLAMP
        e_mid = jnp.exp(jnp.clip(d, -EXP_CLAMP, EXP_CLAMP))
        qs = (q * e_mid).astype(BF16)
        ks = (k / e_mid).astype(BF16)
        qa = (q * jnp.exp(a)).astype(BF16)
        kb = (k * jnp.exp(a_edge - a)).astype(BF16)
        k_s[slot] = k
        a_s[slot] = a

        scores = jnp.where(causal, _dot_nt(qs, ks), 0.0).astype(BF16)
        intra_s[slot] = _dot_nt(scores, vt)
        upd = _dot(vt, jnp.where(own_lanes, jnp.concatenate([kb] * nchunk, axis=1), 0.0))
        inter = [None] * nchunk
        for c in (range(nchunk - 1, -1, -1) if rev else range(nchunk)):
            lo = c * CHUNK
            inter[c] = _dot_nt(qa[lo:lo + CHUNK], st.astype(BF16))
            st = st * jnp.exp(a[edge(c):edge(c) + 1, :]) + upd[:, c * LANES:(c + 1) * LANES]
        return jnp.concatenate(inter, axis=0), overflow, st

    def exact_intra(direction, slot, r0):
        rev = direction == 1
        v_s[...] = vt_ref[:, pl.ds(r0, tblk)].astype(F32).T
        for c in range(nchunk):
            lo = c * CHUNK
            q_c = q_ref[pl.ds(r0 + lo, CHUNK), :].astype(F32)
            a_c = a_s[slot, lo:lo + CHUNK, :]

            def pair(s, acc):
                w = q_c * k_s[slot, pl.ds(lo + s, 1), :] * jnp.exp(
                    jnp.minimum(a_c - a_s[slot, pl.ds(lo + s, 1), :], 0.0))
                r = jnp.sum(w, axis=1, keepdims=True)
                m = (row_c <= s) if rev else (row_c >= s)
                return acc + jnp.where(m, r, 0.0) * v_s[pl.ds(lo + s, 1), :]

            intra_s[slot, lo:lo + CHUNK, :] = lax.fori_loop(
                0, CHUNK, pair, jnp.zeros((CHUNK, LANES), F32))

    def emit(r0, o):
        var = jnp.mean(o * o, axis=-1, keepdims=True)
        gate = g_ref[pl.ds(r0, tblk), :].astype(F32)
        o = o * lax.rsqrt(var + EPS) * ow_ref[...] * (gate * _sigmoid(gate))
        o_ref[pl.ds(r0, tblk), :] = o.astype(o_ref.dtype)

    def sweep(second_half):
        def body(i, carry):
            st = list(carry)
            work = []
            for u in range(HGRN_GROUP):
                blk = i * HGRN_GROUP + u
                for direction in (0, 1):
                    r0 = pl.multiple_of((blk if direction == 0 else nblk - 1 - blk) * tblk, tblk)
                    slot = 2 * u + direction
                    inter, overflow, st[direction] = fast_block(direction, slot, r0, st[direction])
                    work.append((direction, slot, r0, inter, overflow))

            for direction, slot, r0, _, overflow in work:
                @pl.when(overflow)
                def _(direction=direction, slot=slot, r0=r0):
                    exact_intra(direction, slot, r0)

            for _, slot, r0, inter, _ in work:
                o = inter + intra_s[slot]
                if second_half:
                    emit(r0, o + oacc_s[pl.ds(r0, tblk), :])
                else:
                    oacc_s[pl.ds(r0, tblk), :] = o
            return tuple(st)
        return body

    zero = jnp.zeros((LANES, LANES), F32)
    steps = nblk // HGRN_GROUP
    carry = lax.fori_loop(0, steps // 2, sweep(False), (zero, zero))
    lax.fori_loop(steps // 2, steps, sweep(True), carry)


def _hgrn(pb, pvt, pz, lb_logits, onorm_w, *, layer, batch, seq, tblk=256):
    depth = lb_logits.shape[0] // 2
    n = batch * seq
    tblk = min(tblk, seq)

    def slab(base):
        return pl.BlockSpec((None, seq, LANES), lambda b, h: (base + h, b, 0))

    assert seq % (2 * HGRN_GROUP * tblk) == 0
    blocks = 4 * _nbytes((seq, LANES), BF16) + 2 * _nbytes((seq, LANES), F32)
    scratch = _nbytes((seq, LANES), F32) + (1 + 6 * HGRN_GROUP) * _nbytes((tblk, LANES), F32)
    return pl.pallas_call(
        functools.partial(_hgrn_kernel, layer=layer, depth=depth, tblk=tblk),
        grid=(batch, HEADS),
        in_specs=[
            slab(0),
            pl.BlockSpec((None, LANES, seq), lambda b, h: (h, 0, b)),
            slab(HEADS),
            pl.BlockSpec((None, seq, LANES), lambda b, h: (h, b, 0)),
            pl.BlockSpec((None, seq, LANES), lambda b, h: (HEADS + h, b, 0)),
            pl.BlockSpec((2 * depth, LANES), lambda b, h: (0, h)),
            pl.BlockSpec((1, LANES), lambda b, h: (0, 0)),
        ],
        out_specs=pl.BlockSpec((None, seq, LANES), lambda b, h: (h, b, 0)),
        out_shape=jax.ShapeDtypeStruct((HEADS, n, LANES), BF16),
        scratch_shapes=[
            pltpu.VMEM((seq, LANES), F32),
            pltpu.VMEM((2 * HGRN_GROUP, tblk, LANES), F32),
            pltpu.VMEM((2 * HGRN_GROUP, tblk, LANES), F32),
            pltpu.VMEM((tblk, LANES), F32),
            pltpu.VMEM((2 * HGRN_GROUP, tblk, LANES), F32),
        ],
        compiler_params=pltpu.CompilerParams(
            dimension_semantics=("parallel", "parallel"),
            vmem_limit_bytes=_vmem_limit(blocks, scratch)),
        name="hgrn",
    )(pb, pvt, pb, pz, pz, lb_logits, onorm_w.reshape(1, LANES))


HALF = LANES // 2
LANE_SHIFT = HALF
LANE_THI = HALF + 1
LANE_TLO = HALF + 2
LANE_SHI = HALF + 3
LANE_SLO = HALF + 4
POS_SPLIT = 256
ROWSUM_MIN = 1e-24
ROWSUM_MAX = 1e30
ATTN_GROUP = 2
ONES_ROWS = 16


def _half_rms_scale(x):
    sq = x * x
    hi = sq.astype(BF16)
    lo = (sq - hi.astype(F32)).astype(BF16)
    r = lax.broadcasted_iota(jnp.int32, (LANES, LANES), 0) // HALF
    c = lax.broadcasted_iota(jnp.int32, (LANES, LANES), 1) // HALF
    ones = jnp.where(r == c, 1.0, 0.0).astype(BF16)
    return lax.rsqrt((_dot(hi, ones) + _dot(lo, ones)) / HALF + EPS)


def _split_pos(pos):
    return (pos & ~(POS_SPLIT - 1)).astype(F32), (pos & (POS_SPLIT - 1)).astype(F32)


def _attn_kernel(q_ref, k_ref, vt_ref, qw_ref, kw_ref, lam_ref, swc_ref, o_ref,
                 k1_s, k2_s, va_s, kpos_s, qpos_s, dist_s, q_s, bias_s, acc_s, m_s, l_s, accx_s,
                 *, tk, lam_init):
    tq = q_ref.shape[0] // ATTN_GROUP
    seq = k_ref.shape[0]
    nk = seq // tk
    nvar = tk // tq
    h = pl.program_id(1)
    qi = pl.program_id(2)
    lane = lax.broadcasted_iota(jnp.int32, (1, LANES), 1)
    lo_mask = lane < HALF
    slope = lax.bitcast_convert_type(jnp.full((1, LANES), (126 - h) << 23, jnp.int32), F32)

    @pl.when((pl.program_id(0) == 0) & (h == 0) & (qi == 0))
    def _():
        hi, lo = _split_pos(lax.broadcasted_iota(jnp.int32, (seq, LANES), 0))
        kpos_s[...] = jnp.where(lane == LANE_SHI, hi, jnp.where(
            lane == LANE_SLO, lo, jnp.where((lane >= LANE_SHIFT) & (lane <= LANE_TLO), 1.0, 0.0)))
        qpos_s[...] = jnp.where(lane == LANE_THI, -hi, jnp.where(
            lane == LANE_TLO, -lo, jnp.where((lane == LANE_SHI) | (lane == LANE_SLO), 1.0, 0.0)))
        rel = (lax.broadcasted_iota(jnp.int32, (tk, tq), 1) - lax.broadcasted_iota(jnp.int32, (tk, tq), 0))
        for v in range(nvar):
            dist_s[v] = jnp.abs(rel + v * tq).astype(F32)

    @pl.when(qi == 0)
    def _():
        for c in range(nk):
            rows = slice(c * tk, (c + 1) * tk)
            kf = k_ref[rows, :].astype(F32)
            kn = kf * _half_rms_scale(kf) * kw_ref[...]
            pos = kpos_s[rows, :]
            k1_s[rows, :] = jnp.where(lo_mask, kn, pos).astype(BF16)
            k2_s[rows, :] = jnp.where(lo_mask, pltpu.roll(kn, HALF, 1), pos).astype(BF16)
        va_s[:LANES, :] = vt_ref[...]
        va_s[LANES:, :] = jnp.ones((ONES_ROWS, seq), BF16)
        for v in range(nvar):
            bias_s[v] = dist_s[v] * (-slope[:, :1])

    def gain_max(w_ref, mask):
        return jnp.max(jnp.where(mask, jnp.abs(w_ref[...]), 0.0), axis=1, keepdims=True)

    m1 = math.sqrt(HALF) * gain_max(qw_ref, lo_mask) * gain_max(kw_ref, lo_mask)
    m2 = math.sqrt(HALF) * gain_max(qw_ref, ~lo_mask) * gain_max(kw_ref, ~lo_mask)

    lam_p = lam_ref[...]
    lam = (jnp.exp(jnp.sum(lam_p[0:1] * lam_p[1:2], axis=1, keepdims=True))
           - jnp.exp(jnp.sum(lam_p[2:3] * lam_p[3:4], axis=1, keepdims=True)) + lam_init)

    work = []
    for u in range(ATTN_GROUP):
        qb = qi * ATTN_GROUP + u
        qf = q_ref[u * tq:(u + 1) * tq, :].astype(F32)
        qn = qf * _half_rms_scale(qf) * qw_ref[...] * (HALF ** -0.5)
        q1 = jnp.where(lo_mask, qn, 0.0)
        q2 = jnp.where(lo_mask, pltpu.roll(qn, HALF, 1), 0.0)
        q1m = jnp.where(lane == LANE_SHIFT, -m1, q1)
        q2m = jnp.where(lane == LANE_SHIFT, -m2, q2)
        pen = qpos_s[pl.ds(pl.multiple_of(qb * tq, tq), tq), :] * slope
        for v, sigma in enumerate((1.0, -1.0, 0.0)):
            q_s[6 * u + v] = (q1m + sigma * pen).astype(BF16)
            q_s[6 * u + 3 + v] = (q2m + sigma * pen).astype(BF16)
        work.append((u, qb, q1, q2))

    for u, qb, _, _ in work:
        kjm = (qb * tq) // tk
        diag_tile = (qb * tq - kjm * tk) // tq
        for r in range(nk):
            if r == 0:
                kj, ver = kjm, 2
            else:
                kj = kjm + r
                kj = jnp.where(kj >= nk, kj - nk, kj)
                ver = jnp.where(kj < kjm, 0, 1)
            k0 = pl.multiple_of(kj * tk, tk)
            s1 = _dot_nt(k1_s[pl.ds(k0, tk), :], q_s[6 * u + ver])
            s2 = _dot_nt(k2_s[pl.ds(k0, tk), :], q_s[6 * u + 3 + ver])
            if r == 0:
                bias = bias_s[diag_tile]
                s1, s2 = s1 + bias, s2 + bias
            vv = va_s[:, pl.ds(k0, tk)]
            p1 = jnp.exp(s1).astype(BF16)
            p2 = jnp.exp(s2).astype(BF16)
            if r == 0:
                acc_s[2 * u] = _dot(vv, p1)
                acc_s[2 * u + 1] = _dot(vv, p2)
            else:
                acc_s[2 * u] += _dot(vv, p1)
                acc_s[2 * u + 1] += _dot(vv, p2)

    def finish(u, o1t, o2t):
        ot = o1t - lam * o2t
        var = jnp.mean(ot * ot, axis=0, keepdims=True)
        ot = ot * lax.rsqrt(var + EPS) * swc_ref[...] * (1.0 - lam_init)
        o_ref[u * tq:(u + 1) * tq, :] = ot.T.astype(o_ref.dtype)

    unsafe = []
    for u, _, _, _ in work:
        a1 = acc_s[2 * u]
        a2 = acc_s[2 * u + 1]
        l1 = a1[LANES:LANES + 1, :]
        l2 = a2[LANES:LANES + 1, :]
        finish(u, a1[:LANES] / l1, a2[:LANES] / l2)
        lmin = jnp.minimum(jnp.min(l1), jnp.min(l2))
        lmax = jnp.maximum(jnp.max(l1), jnp.max(l2))
        unsafe.append(jnp.logical_not((lmin >= ROWSUM_MIN) & (lmax <= ROWSUM_MAX)))

    for (u, qb, q1, q2), redo in zip(work, unsafe):
        @pl.when(redo)
        def _(u=u, qb=qb, q1=q1, q2=q2):
            q_both = jnp.concatenate([q1, q2], axis=0).astype(BF16)
            m_s[...] = jnp.full(m_s.shape, -jnp.inf, F32)
            l_s[...] = jnp.zeros(l_s.shape, F32)
            accx_s[...] = jnp.zeros(accx_s.shape, F32)
            rel = (lax.broadcasted_iota(jnp.int32, (tq, tk), 0) - lax.broadcasted_iota(jnp.int32, (tq, tk), 1)
                   + qb * tq).astype(F32)

            def exact_body(kj, carry):
                k0 = pl.multiple_of(kj * tk, tk)
                s = jnp.concatenate([_dot_nt(q_both[:tq], k1_s[pl.ds(k0, tk), :]),
                                     _dot_nt(q_both[tq:], k2_s[pl.ds(k0, tk), :])], axis=0)
                pen = jnp.abs(rel - lax.convert_element_type(kj * tk, F32)) * (-slope[:, :1])
                s = s + jnp.concatenate([pen, pen], axis=0)
                m_prev = m_s[...]
                m_new = jnp.maximum(m_prev, jnp.max(s, axis=1, keepdims=True))
                alpha = jnp.exp(m_prev - m_new)
                p = jnp.exp(s - m_new)
                l_s[...] = alpha * l_s[...] + jnp.sum(p, axis=1, keepdims=True)
                accx_s[...] = alpha * accx_s[...] + _dot_nt(p.astype(BF16), va_s[:LANES, pl.ds(k0, tk)])
                m_s[...] = m_new
                return carry

            lax.fori_loop(0, nk, exact_body, 0)
            acc = accx_s[...]
            l = l_s[...]
            finish(u, (acc[:tq] / l[:tq]).T, (acc[tq:] / l[tq:]).T)


def _attn(pb, pvt, qn_w, kn_w, lam_p, subln_w, *, layer, batch, seq, tq=512, tk=512):
    n = batch * seq
    tq = min(tq, seq)
    tk = min(tk, seq)
    nq = seq // (ATTN_GROUP * tq)
    assert tk % tq == 0 and seq % tk == 0 and seq % (ATTN_GROUP * tq) == 0 and seq <= POS_SPLIT * POS_SPLIT
    lam_init = 0.8 - 0.6 * math.exp(-0.3 * layer)
    blocks = 2 * _nbytes((ATTN_GROUP * tq, LANES), BF16) + 2 * _nbytes((seq, LANES), BF16)
    scratch = (3 * _nbytes((seq, LANES), BF16) + _nbytes((ONES_ROWS, seq), BF16) + 2 * _nbytes((seq, LANES), F32)
               + 6 * ATTN_GROUP * _nbytes((tq, LANES), BF16)
               + (1 + 2 * (tk // tq)) * _nbytes((tk, tq), F32)
               + 2 * ATTN_GROUP * _nbytes((LANES + ONES_ROWS, tq), F32)
               + 3 * _nbytes((2 * tq, LANES), F32) + 4 * _nbytes((tk, tq), F32))
    return pl.pallas_call(
        functools.partial(_attn_kernel, tk=tk, lam_init=lam_init),
        grid=(batch, HEADS, nq),
        in_specs=[
            pl.BlockSpec((None, ATTN_GROUP * tq, LANES), lambda b, h, i: (2 * HEADS + h, b * nq + i, 0)),
            pl.BlockSpec((None, seq, LANES), lambda b, h, i: (3 * HEADS + h, b, 0)),
            pl.BlockSpec((None, LANES, seq), lambda b, h, i: (HEADS + h, 0, b)),
            pl.BlockSpec((1, LANES), lambda b, h, i: (0, 0)),
            pl.BlockSpec((1, LANES), lambda b, h, i: (0, 0)),
            pl.BlockSpec((4, LANES // 2), lambda b, h, i: (0, 0)),
            pl.BlockSpec((LANES, 1), lambda b, h, i: (0, 0)),
        ],
        out_specs=pl.BlockSpec((None, ATTN_GROUP * tq, LANES), lambda b, h, i: (h, b * nq + i, 0)),
        out_shape=jax.ShapeDtypeStruct((HEADS, n, LANES), BF16),
        scratch_shapes=[
            pltpu.VMEM((seq, LANES), BF16),
            pltpu.VMEM((seq, LANES), BF16),
            pltpu.VMEM((LANES + ONES_ROWS, seq), BF16),
            pltpu.VMEM((seq, LANES), F32),
            pltpu.VMEM((seq, LANES), F32),
            pltpu.VMEM((tk // tq, tk, tq), F32),
            pltpu.VMEM((6 * ATTN_GROUP, tq, LANES), BF16),
            pltpu.VMEM((tk // tq, tk, tq), F32),
            pltpu.VMEM((2 * ATTN_GROUP, LANES + ONES_ROWS, tq), F32),
            pltpu.VMEM((2 * tq, 1), F32),
            pltpu.VMEM((2 * tq, 1), F32),
            pltpu.VMEM((2 * tq, LANES), F32),
        ],
        compiler_params=pltpu.CompilerParams(
            dimension_semantics=("arbitrary", "arbitrary", "arbitrary"),
            vmem_limit_bytes=_vmem_limit(blocks, scratch)),
        name="diffattn",
    )(pb, pb, pvt, jnp.tile(qn_w, 2).reshape(1, LANES), jnp.tile(kn_w, 2).reshape(1, LANES),
      lam_p, subln_w.reshape(LANES, 1))


def _outproj_kernel(ohg_ref, oda_ref, x_ref, w_ref, o_ref, mix_s):
    for c in range(HEADS):
        mix_s[:, c * LANES:(c + 1) * LANES] = ohg_ref[c]
        mix_s[:, (HEADS + c) * LANES:(HEADS + c + 1) * LANES] = oda_ref[c]
    o_ref[...] = x_ref[...] + _dot(mix_s[...], w_ref[...])


def _outproj(ohg, oda, x, w, layer, *, tm=512):
    n, d = x.shape
    width = w.shape[1]
    tm = min(tm, n)
    blocks = (2 * _nbytes((HEADS, tm, LANES), BF16) + 2 * _nbytes((tm, d), F32) + _nbytes((width, d), BF16))
    return pl.pallas_call(
        _outproj_kernel,
        grid=(n // tm,),
        in_specs=[
            pl.BlockSpec((HEADS, tm, LANES), lambda i: (0, i, 0)),
            pl.BlockSpec((HEADS, tm, LANES), lambda i: (0, i, 0)),
            pl.BlockSpec((tm, d), lambda i: (i, 0)),
            pl.BlockSpec((None, width, d), lambda i: (layer, 0, 0)),
        ],
        out_specs=pl.BlockSpec((tm, d), lambda i: (i, 0)),
        out_shape=jax.ShapeDtypeStruct((n, d), F32),
        scratch_shapes=[pltpu.VMEM((tm, width), BF16)],
        compiler_params=pltpu.CompilerParams(
            dimension_semantics=("parallel",),
            vmem_limit_bytes=_vmem_limit(blocks, _nbytes((tm, width), BF16))),
        name="outproj",
    )(ohg, oda, x, w)


def _ffn_kernel(x_ref, nw_ref, wg_ref, wu_ref, wd_ref, o_ref, h_s):
    @pl.when(pl.program_id(1) == 0)
    def _():
        x = x_ref[...]
        var = jnp.mean(x * x, axis=-1, keepdims=True)
        h_s[...] = (x * lax.rsqrt(var + EPS) * nw_ref[...]).astype(BF16)
        o_ref[...] = x

    h = h_s[...]
    g = _dot(h, wg_ref[...])
    u = _dot(h, wu_ref[...])
    o_ref[...] += _dot((g * _sigmoid(g) * u).astype(BF16), wd_ref[...])


def _ffn(x, nw, wg, wu, wd, layer, *, tm=1024, tf=512):
    n, d = x.shape
    ff = wg.shape[2]
    tm = min(tm, n)
    blocks = (2 * _nbytes((tm, d), F32) + 3 * _nbytes((d, tf), BF16))
    scratch = _nbytes((tm, d), BF16) + 3 * _nbytes((tm, tf), F32)
    return pl.pallas_call(
        _ffn_kernel,
        grid=(n // tm, ff // tf),
        in_specs=[
            pl.BlockSpec((tm, d), lambda i, f: (jnp.minimum(i + jnp.minimum(f, 1), n // tm - 1), 0)),
            pl.BlockSpec((1, d), lambda i, f: (0, 0)),
            pl.BlockSpec((None, d, tf), lambda i, f: (layer, 0, f)),
            pl.BlockSpec((None, d, tf), lambda i, f: (layer, 0, f)),
            pl.BlockSpec((None, tf, d), lambda i, f: (layer, f, 0)),
        ],
        out_specs=pl.BlockSpec((tm, d), lambda i, f: (i, 0)),
        out_shape=jax.ShapeDtypeStruct((n, d), F32),
        scratch_shapes=[pltpu.VMEM((tm, d), BF16)],
        compiler_params=pltpu.CompilerParams(
            dimension_semantics=("parallel", "arbitrary"),
            vmem_limit_bytes=_vmem_limit(blocks, scratch)),
        name="ffn",
    )(x, nw.reshape(1, d), wg, wu, wd)


def kernel(x, norm_mix_w, w_in, hg_lb_logits, hg_onorm_w, da_qnorm_w, da_knorm_w, da_lambda,
           da_subln_w, w_out, norm_ffn_w, w_gate, w_up, w_down):
    batch, seq, d = x.shape
    depth = w_in.shape[0]
    hg = d // 2
    assert hg == HEADS * LANES and seq % CHUNK == 0
    xf = x.reshape(batch * seq, d).astype(F32)
    lb_logits = hg_lb_logits.astype(F32).reshape(2 * depth, hg)

    tn = hg
    w_in_bf = w_in.astype(BF16)
    per_tile = tn // WT_TILE
    w_in_t = _transposed_columns(w_in, tuple(3 * per_tile + c for c in range(per_tile))
                                 + tuple(7 * per_tile + c for c in range(per_tile)))
    w_out_bf, w_gate_bf, w_up_bf, w_down_bf = (w.astype(BF16) for w in (w_out, w_gate, w_up, w_down))

    for l in range(depth):
        pb, pz, pvt = _proj(xf, norm_mix_w[l], w_in_bf, w_in_t, l, (0, 4, 5, 6), (1, 2), tn=tn)
        ohg = _hgrn(pb, pvt, pz, lb_logits, hg_onorm_w[l], layer=l, batch=batch, seq=seq)
        oda = _attn(pb, pvt, da_qnorm_w[l], da_knorm_w[l], da_lambda[l].astype(F32), da_subln_w[l],
                    layer=l, batch=batch, seq=seq)
        x1 = _outproj(ohg, oda, xf, w_out_bf, l)
        xf = _ffn(x1, norm_ffn_w[l], w_gate_bf, w_up_bf, w_down_bf, l)
    return xf.reshape(batch, seq, d).astype(x.dtype)
```

```python
import functools
import math

import jax
import jax.numpy as jnp
from jax import lax
from jax.experimental import pallas as pl
from jax.experimental.pallas import tpu as pltpu

F32 = jnp.float32
BF16 = jnp.bfloat16

LANES = 128
HEADS = 8
EPS = 1e-6
LB_FLOOR = 1e-30
LB_CEIL = 1.0 - 1e-6
CHUNK = 128
HGRN_GROUP = 4
EXP_CLAMP = 80.0
VMEM_CAP = 60 * 1024 * 1024
VMEM_SLACK = 12 * 1024 * 1024


def _vmem_limit(pipelined_bytes, scratch_bytes=0):
    return int(min(VMEM_CAP, 2 * pipelined_bytes + scratch_bytes + VMEM_SLACK))


def _nbytes(shape, dtype):
    return math.prod(shape) * jnp.dtype(dtype).itemsize


def _dot(a, b):
    return jnp.dot(a, b, preferred_element_type=F32)


def _dot_nt(a, b):
    return lax.dot_general(a, b, (((1,), (1,)), ((), ())), preferred_element_type=F32)


def _sigmoid(x):
    return 1.0 / (1.0 + jnp.exp(-x))


def _proj_kernel(x_ref, nw_ref, w_ref, wt_ref, ob_ref, oz_ref, ot_ref, h_s, *, nb, nz):
    j = pl.program_id(1)

    @pl.when(j == 0)
    def _():
        x = x_ref[...]
        var = jnp.mean(x * x, axis=-1, keepdims=True)
        h_s[...] = (x * lax.rsqrt(var + EPS) * nw_ref[...]).astype(BF16)

    def slabs(o_ref):
        r = _dot(h_s[...], w_ref[...])
        for c in range(o_ref.shape[0]):
            o_ref[c] = r[:, c * LANES:(c + 1) * LANES].astype(o_ref.dtype)

    @pl.when(j < nb)
    def _():
        slabs(ob_ref)

    @pl.when((j >= nb) & (j < nb + nz))
    def _():
        slabs(oz_ref)

    @pl.when(j >= nb + nz)
    def _():
        r = _dot_nt(wt_ref[...], h_s[...])
        for c in range(ot_ref.shape[0]):
            ot_ref[c] = r[c * LANES:(c + 1) * LANES, :].astype(ot_ref.dtype)


def _pick(j, values):
    out = values[-1]
    for idx in range(len(values) - 2, -1, -1):
        out = jnp.where(j <= idx, values[idx], out)
    return out


def _wt_kernel(w_ref, o_ref):
    o_ref[...] = w_ref[...].T.astype(o_ref.dtype)


WT_TILE = 512


def _transposed_columns(w, col_tiles, *, t=WT_TILE):
    depth, d, _ = w.shape
    return pl.pallas_call(
        _wt_kernel,
        grid=(depth, len(col_tiles), d // t),
        in_specs=[pl.BlockSpec((None, t, t), lambda l, c, r: (l, r, _pick(c, col_tiles)))],
        out_specs=pl.BlockSpec((None, t, t), lambda l, c, r: (l, c, r)),
        out_shape=jax.ShapeDtypeStruct((depth, len(col_tiles) * t, d), BF16),
        compiler_params=pltpu.CompilerParams(
            dimension_semantics=("parallel", "parallel", "parallel"),
            vmem_limit_bytes=_vmem_limit(_nbytes((t, t), F32) + _nbytes((t, t), BF16), 2 * _nbytes((t, t), F32))),
        name="wtranspose",
    )(w)


def _proj(x, nw, w, wt, layer, tiles_b, tiles_z, *, tm=1024, tn=1024):
    n, d = x.shape
    tm = min(tm, n)
    nb, nz, nt = len(tiles_b), len(tiles_z), wt.shape[1] // tn
    assert nt * tn == wt.shape[1] and w.shape[2] % tn == 0
    per = tn // LANES
    main_tiles = tuple(tiles_b) + tuple(tiles_z)
    blocks = (_nbytes((tm, d), F32) + 2 * _nbytes((d, tn), BF16) + _nbytes((tm, tn), BF16)
              + _nbytes((tm, tn), F32) + _nbytes((tm, tn), BF16))
    return pl.pallas_call(
        functools.partial(_proj_kernel, nb=nb, nz=nz),
        grid=(n // tm, nb + nz + nt),
        in_specs=[
            pl.BlockSpec((tm, d), lambda i, j: (jnp.minimum(i + jnp.minimum(j, 1), n // tm - 1), 0)),
            pl.BlockSpec((1, d), lambda i, j: (0, 0)),
            pl.BlockSpec((None, d, tn), lambda i, j: (layer, 0, _pick(j, main_tiles))),
            pl.BlockSpec((None, tn, d), lambda i, j: (layer, jnp.clip(j - nb - nz, 0, nt - 1), 0)),
        ],
        out_specs=[
            pl.BlockSpec((per, tm, LANES), lambda i, j: (jnp.minimum(j, nb - 1), i, 0)),
            pl.BlockSpec((per, tm, LANES), lambda i, j: (jnp.clip(j - nb, 0, nz - 1), i, 0)),
            pl.BlockSpec((per, LANES, tm), lambda i, j: (jnp.clip(j - nb - nz, 0, nt - 1), 0, i)),
        ],
        out_shape=[
            jax.ShapeDtypeStruct((nb * per, n, LANES), BF16),
            jax.ShapeDtypeStruct((nz * per, n, LANES), F32),
            jax.ShapeDtypeStruct((nt * per, LANES, n), BF16),
        ],
        scratch_shapes=[pltpu.VMEM((tm, d), BF16)],
        compiler_params=pltpu.CompilerParams(
            dimension_semantics=("parallel", "arbitrary"),
            vmem_limit_bytes=_vmem_limit(blocks, _nbytes((tm, d), BF16) + _nbytes((tm, tn), F32))),
        name="proj",
    )(x, nw.reshape(1, d), w, wt)


def _seg_cumsum(x, seg, rev):
    n = x.shape[0]
    pos = lax.broadcasted_iota(jnp.int32, x.shape, 0) & (seg - 1)
    s = 1
    while s < seg:
        if rev:
            x = x + jnp.where(pos < seg - s, pltpu.roll(x, n - s, 0), 0.0)
        else:
            x = x + jnp.where(pos >= s, pltpu.roll(x, s, 0), 0.0)
        s *= 2
    return x


def _rows_bcast(x, rows, reps):
    return jnp.concatenate([jnp.broadcast_to(x[r:r + 1, :], (reps, x.shape[1])) for r in rows], axis=0)


def _hgrn_kernel(q_ref, vt_ref, g_ref, zf_ref, zb_ref, lbl_ref, ow_ref, o_ref,
                 oacc_s, k_s, a_s, v_s, intra_s, *, layer, depth, tblk):
    seq = q_ref.shape[0]
    nblk = seq // tblk
    nchunk = tblk // CHUNK

    def lower_bound(direction):
        lg = lbl_ref[direction * depth:(direction + 1) * depth, :]
        e = jnp.exp(lg - jnp.max(lg, axis=0, keepdims=True))
        p = e / jnp.sum(e, axis=0, keepdims=True)
        lb = jnp.sum(p[:layer + 1], axis=0, keepdims=True) - p[0:1]
        lb = jnp.clip(lb, 0.0, LB_CEIL)
        return lb, jnp.maximum(lb, LB_FLOOR)

    row_c = lax.broadcasted_iota(jnp.int32, (CHUNK, LANES), 0)
    sr = lax.broadcasted_iota(jnp.int32, (tblk, tblk), 0)
    sc_ = lax.broadcasted_iota(jnp.int32, (tblk, tblk), 1)
    same_chunk = (sr // CHUNK) == (sc_ // CHUNK)
    own_lanes = (lax.broadcasted_iota(jnp.int32, (tblk, nchunk * LANES), 0) // CHUNK
                 == lax.broadcasted_iota(jnp.int32, (tblk, nchunk * LANES), 1) // LANES)

    bounds = (lower_bound(0), lower_bound(1))
    z_refs = (zf_ref, zb_ref)

    def fast_block(direction, slot, r0, st):
        rev = direction == 1
        lb, lbp = bounds[direction]
        causal = same_chunk & ((sc_ >= sr) if rev else (sc_ <= sr))
        q = q_ref[pl.ds(r0, tblk), :].astype(F32)
        vt = vt_ref[:, pl.ds(r0, tblk)]
        z = z_refs[direction][pl.ds(r0, tblk), :]
        sg = _sigmoid(z)
        lf = jnp.log(lbp + (1.0 - lb) * sg)
        k = (1.0 - lb) * (1.0 - sg) - (lbp - lb)
        a = _seg_cumsum(lf, CHUNK, rev)
        edge = (lambda c: c * CHUNK) if rev else (lambda c: c * CHUNK + CHUNK - 1)
        a_edge = _rows_bcast(a, [edge(c) for c in range(nchunk)], CHUNK)
        d = a - _rows_bcast(a, [c * CHUNK + CHUNK // 2 for c in range(nchunk)], CHUNK)
        overflow = jnp.max(jnp.abs(d)) > EXP_CLAMP
        e_mid = jnp.exp(jnp.clip(d, -EXP_CLAMP, EXP_CLAMP))
        qs = (q * e_mid).astype(BF16)
        ks = (k / e_mid).astype(BF16)
        qa = (q * jnp.exp(a)).astype(BF16)
        kb = (k * jnp.exp(a_edge - a)).astype(BF16)
        k_s[slot] = k
        a_s[slot] = a

        scores = jnp.where(causal, _dot_nt(qs, ks), 0.0).astype(BF16)
        intra_s[slot] = _dot_nt(scores, vt)
        upd = _dot(vt, jnp.where(own_lanes, jnp.concatenate([kb] * nchunk, axis=1), 0.0))
        inter = [None] * nchunk
        for c in (range(nchunk - 1, -1, -1) if rev else range(nchunk)):
            lo = c * CHUNK
            inter[c] = _dot_nt(qa[lo:lo + CHUNK], st.astype(BF16))
            st = st * jnp.exp(a[edge(c):edge(c) + 1, :]) + upd[:, c * LANES:(c + 1) * LANES]
        return jnp.concatenate(inter, axis=0), overflow, st

    def exact_intra(direction, slot, r0):
        rev = direction == 1
        v_s[...] = vt_ref[:, pl.ds(r0, tblk)].astype(F32).T
        for c in range(nchunk):
            lo = c * CHUNK
            q_c = q_ref[pl.ds(r0 + lo, CHUNK), :].astype(F32)
            a_c = a_s[slot, lo:lo + CHUNK, :]

            def pair(s, acc):
                w = q_c * k_s[slot, pl.ds(lo + s, 1), :] * jnp.exp(
                    jnp.minimum(a_c - a_s[slot, pl.ds(lo + s, 1), :], 0.0))
                r = jnp.sum(w, axis=1, keepdims=True)
                m = (row_c <= s) if rev else (row_c >= s)
                return acc + jnp.where(m, r, 0.0) * v_s[pl.ds(lo + s, 1), :]

            intra_s[slot, lo:lo + CHUNK, :] = lax.fori_loop(
                0, CHUNK, pair, jnp.zeros((CHUNK, LANES), F32))

    def emit(r0, o):
        var = jnp.mean(o * o, axis=-1, keepdims=True)
        gate = g_ref[pl.ds(r0, tblk), :].astype(F32)
        o = o * lax.rsqrt(var + EPS) * ow_ref[...] * (gate * _sigmoid(gate))
        o_ref[pl.ds(r0, tblk), :] = o.astype(o_ref.dtype)

    def sweep(second_half):
        def body(i, carry):
            st = list(carry)
            work = []
            for u in range(HGRN_GROUP):
                blk = i * HGRN_GROUP + u
                for direction in (0, 1):
                    r0 = pl.multiple_of((blk if direction == 0 else nblk - 1 - blk) * tblk, tblk)
                    slot = 2 * u + direction
                    inter, overflow, st[direction] = fast_block(direction, slot, r0, st[direction])
                    work.append((direction, slot, r0, inter, overflow))

            for direction, slot, r0, _, overflow in work:
                @pl.when(overflow)
                def _(direction=direction, slot=slot, r0=r0):
                    exact_intra(direction, slot, r0)

            for _, slot, r0, inter, _ in work:
                o = inter + intra_s[slot]
                if second_half:
                    emit(r0, o + oacc_s[pl.ds(r0, tblk), :])
                else:
                    oacc_s[pl.ds(r0, tblk), :] = o
            return tuple(st)
        return body

    zero = jnp.zeros((LANES, LANES), F32)
    steps = nblk // HGRN_GROUP
    carry = lax.fori_loop(0, steps // 2, sweep(False), (zero, zero))
    lax.fori_loop(steps // 2, steps, sweep(True), carry)


def _hgrn(pb, pvt, pz, lb_logits, onorm_w, *, layer, batch, seq, tblk=256):
    depth = lb_logits.shape[0] // 2
    n = batch * seq
    tblk = min(tblk, seq)

    def slab(base):
        return pl.BlockSpec((None, seq, LANES), lambda b, h: (base + h, b, 0))

    assert seq % (2 * HGRN_GROUP * tblk) == 0
    blocks = 4 * _nbytes((seq, LANES), BF16) + 2 * _nbytes((seq, LANES), F32)
    scratch = _nbytes((seq, LANES), F32) + (1 + 6 * HGRN_GROUP) * _nbytes((tblk, LANES), F32)
    return pl.pallas_call(
        functools.partial(_hgrn_kernel, layer=layer, depth=depth, tblk=tblk),
        grid=(batch, HEADS),
        in_specs=[
            slab(0),
            pl.BlockSpec((None, LANES, seq), lambda b, h: (h, 0, b)),
            slab(HEADS),
            pl.BlockSpec((None, seq, LANES), lambda b, h: (h, b, 0)),
            pl.BlockSpec((None, seq, LANES), lambda b, h: (HEADS + h, b, 0)),
            pl.BlockSpec((2 * depth, LANES), lambda b, h: (0, h)),
            pl.BlockSpec((1, LANES), lambda b, h: (0, 0)),
        ],
        out_specs=pl.BlockSpec((None, seq, LANES), lambda b, h: (h, b, 0)),
        out_shape=jax.ShapeDtypeStruct((HEADS, n, LANES), BF16),
        scratch_shapes=[
            pltpu.VMEM((seq, LANES), F32),
            pltpu.VMEM((2 * HGRN_GROUP, tblk, LANES), F32),
            pltpu.VMEM((2 * HGRN_GROUP, tblk, LANES), F32),
            pltpu.VMEM((tblk, LANES), F32),
            pltpu.VMEM((2 * HGRN_GROUP, tblk, LANES), F32),
        ],
        compiler_params=pltpu.CompilerParams(
            dimension_semantics=("parallel", "parallel"),
            vmem_limit_bytes=_vmem_limit(blocks, scratch)),
        name="hgrn",
    )(pb, pvt, pb, pz, pz, lb_logits, onorm_w.reshape(1, LANES))


HALF = LANES // 2
LANE_SHIFT = HALF
LANE_THI = HALF + 1
LANE_TLO = HALF + 2
LANE_SHI = HALF + 3
LANE_SLO = HALF + 4
POS_SPLIT = 256
ROWSUM_MIN = 1e-24
ROWSUM_MAX = 1e30
ATTN_GROUP = 2
ONES_ROWS = 16


def _half_rms_scale(x):
    sq = x * x
    hi = sq.astype(BF16)
    lo = (sq - hi.astype(F32)).astype(BF16)
    r = lax.broadcasted_iota(jnp.int32, (LANES, LANES), 0) // HALF
    c = lax.broadcasted_iota(jnp.int32, (LANES, LANES), 1) // HALF
    ones = jnp.where(r == c, 1.0, 0.0).astype(BF16)
    return lax.rsqrt((_dot(hi, ones) + _dot(lo, ones)) / HALF + EPS)


def _split_pos(pos):
    return (pos & ~(POS_SPLIT - 1)).astype(F32), (pos & (POS_SPLIT - 1)).astype(F32)


def _attn_kernel(q_ref, k_ref, vt_ref, qw_ref, kw_ref, lam_ref, swc_ref, o_ref,
                 k1_s, k2_s, va_s, kpos_s, qpos_s, dist_s, q_s, bias_s, acc_s, m_s, l_s, accx_s,
                 *, tk, lam_init):
    tq = q_ref.shape[0] // ATTN_GROUP
    seq = k_ref.shape[0]
    nk = seq // tk
    nvar = tk // tq
    h = pl.program_id(1)
    qi = pl.program_id(2)
    lane = lax.broadcasted_iota(jnp.int32, (1, LANES), 1)
    lo_mask = lane < HALF
    slope = lax.bitcast_convert_type(jnp.full((1, LANES), (126 - h) << 23, jnp.int32), F32)

    @pl.when((pl.program_id(0) == 0) & (h == 0) & (qi == 0))
    def _():
        hi, lo = _split_pos(lax.broadcasted_iota(jnp.int32, (seq, LANES), 0))
        kpos_s[...] = jnp.where(lane == LANE_SHI, hi, jnp.where(
            lane == LANE_SLO, lo, jnp.where((lane >= LANE_SHIFT) & (lane <= LANE_TLO), 1.0, 0.0)))
        qpos_s[...] = jnp.where(lane == LANE_THI, -hi, jnp.where(
            lane == LANE_TLO, -lo, jnp.where((lane == LANE_SHI) | (lane == LANE_SLO), 1.0, 0.0)))
        rel = (lax.broadcasted_iota(jnp.int32, (tk, tq), 1) - lax.broadcasted_iota(jnp.int32, (tk, tq), 0))
        for v in range(nvar):
            dist_s[v] = jnp.abs(rel + v * tq).astype(F32)

    @pl.when(qi == 0)
    def _():
        for c in range(nk):
            rows = slice(c * tk, (c + 1) * tk)
            kf = k_ref[rows, :].astype(F32)
            kn = kf * _half_rms_scale(kf) * kw_ref[...]
            pos = kpos_s[rows, :]
            k1_s[rows, :] = jnp.where(lo_mask, kn, pos).astype(BF16)
            k2_s[rows, :] = jnp.where(lo_mask, pltpu.roll(kn, HALF, 1), pos).astype(BF16)
        va_s[:LANES, :] = vt_ref[...]
        va_s[LANES:, :] = jnp.ones((ONES_ROWS, seq), BF16)
        for v in range(nvar):
            bias_s[v] = dist_s[v] * (-slope[:, :1])

    def gain_max(w_ref, mask):
        return jnp.max(jnp.where(mask, jnp.abs(w_ref[...]), 0.0), axis=1, keepdims=True)

    m1 = math.sqrt(HALF) * gain_max(qw_ref, lo_mask) * gain_max(kw_ref, lo_mask)
    m2 = math.sqrt(HALF) * gain_max(qw_ref, ~lo_mask) * gain_max(kw_ref, ~lo_mask)

    lam_p = lam_ref[...]
    lam = (jnp.exp(jnp.sum(lam_p[0:1] * lam_p[1:2], axis=1, keepdims=True))
           - jnp.exp(jnp.sum(lam_p[2:3] * lam_p[3:4], axis=1, keepdims=True)) + lam_init)

    work = []
    for u in range(ATTN_GROUP):
        qb = qi * ATTN_GROUP + u
        qf = q_ref[u * tq:(u + 1) * tq, :].astype(F32)
        qn = qf * _half_rms_scale(qf) * qw_ref[...] * (HALF ** -0.5)
        q1 = jnp.where(lo_mask, qn, 0.0)
        q2 = jnp.where(lo_mask, pltpu.roll(qn, HALF, 1), 0.0)
        q1m = jnp.where(lane == LANE_SHIFT, -m1, q1)
        q2m = jnp.where(lane == LANE_SHIFT, -m2, q2)
        pen = qpos_s[pl.ds(pl.multiple_of(qb * tq, tq), tq), :] * slope
        for v, sigma in enumerate((1.0, -1.0, 0.0)):
            q_s[6 * u + v] = (q1m + sigma * pen).astype(BF16)
            q_s[6 * u + 3 + v] = (q2m + sigma * pen).astype(BF16)
        work.append((u, qb, q1, q2))

    for u, qb, _, _ in work:
        kjm = (qb * tq) // tk
        diag_tile = (qb * tq - kjm * tk) // tq
        for r in range(nk):
            if r == 0:
                kj, ver = kjm, 2
            else:
                kj = kjm + r
                kj = jnp.where(kj >= nk, kj - nk, kj)
                ver = jnp.where(kj < kjm, 0, 1)
            k0 = pl.multiple_of(kj * tk, tk)
            s1 = _dot_nt(k1_s[pl.ds(k0, tk), :], q_s[6 * u + ver])
            s2 = _dot_nt(k2_s[pl.ds(k0, tk), :], q_s[6 * u + 3 + ver])
            if r == 0:
                bias = bias_s[diag_tile]
                s1, s2 = s1 + bias, s2 + bias
            vv = va_s[:, pl.ds(k0, tk)]
            p1 = jnp.exp(s1).astype(BF16)
            p2 = jnp.exp(s2).astype(BF16)
            if r == 0:
                acc_s[2 * u] = _dot(vv, p1)
                acc_s[2 * u + 1] = _dot(vv, p2)
            else:
                acc_s[2 * u] += _dot(vv, p1)
                acc_s[2 * u + 1] += _dot(vv, p2)

    def finish(u, o1t, o2t):
        ot = o1t - lam * o2t
        var = jnp.mean(ot * ot, axis=0, keepdims=True)
        ot = ot * lax.rsqrt(var + EPS) * swc_ref[...] * (1.0 - lam_init)
        o_ref[u * tq:(u + 1) * tq, :] = ot.T.astype(o_ref.dtype)

    unsafe = []
    for u, _, _, _ in work:
        a1 = acc_s[2 * u]
        a2 = acc_s[2 * u + 1]
        l1 = a1[LANES:LANES + 1, :]
        l2 = a2[LANES:LANES + 1, :]
        finish(u, a1[:LANES] / l1, a2[:LANES] / l2)
        lmin = jnp.minimum(jnp.min(l1), jnp.min(l2))
        lmax = jnp.maximum(jnp.max(l1), jnp.max(l2))
        unsafe.append(jnp.logical_not((lmin >= ROWSUM_MIN) & (lmax <= ROWSUM_MAX)))

    for (u, qb, q1, q2), redo in zip(work, unsafe):
        @pl.when(redo)
        def _(u=u, qb=qb, q1=q1, q2=q2):
            q_both = jnp.concatenate([q1, q2], axis=0).astype(BF16)
            m_s[...] = jnp.full(m_s.shape, -jnp.inf, F32)
            l_s[...] = jnp.zeros(l_s.shape, F32)
            accx_s[...] = jnp.zeros(accx_s.shape, F32)
            rel = (lax.broadcasted_iota(jnp.int32, (tq, tk), 0) - lax.broadcasted_iota(jnp.int32, (tq, tk), 1)
                   + qb * tq).astype(F32)

            def exact_body(kj, carry):
                k0 = pl.multiple_of(kj * tk, tk)
                s = jnp.concatenate([_dot_nt(q_both[:tq], k1_s[pl.ds(k0, tk), :]),
                                     _dot_nt(q_both[tq:], k2_s[pl.ds(k0, tk), :])], axis=0)
                pen = jnp.abs(rel - lax.convert_element_type(kj * tk, F32)) * (-slope[:, :1])
                s = s + jnp.concatenate([pen, pen], axis=0)
                m_prev = m_s[...]
                m_new = jnp.maximum(m_prev, jnp.max(s, axis=1, keepdims=True))
                alpha = jnp.exp(m_prev - m_new)
                p = jnp.exp(s - m_new)
                l_s[...] = alpha * l_s[...] + jnp.sum(p, axis=1, keepdims=True)
                accx_s[...] = alpha * accx_s[...] + _dot_nt(p.astype(BF16), va_s[:LANES, pl.ds(k0, tk)])
                m_s[...] = m_new
                return carry

            lax.fori_loop(0, nk, exact_body, 0)
            acc = accx_s[...]
            l = l_s[...]
            finish(u, (acc[:tq] / l[:tq]).T, (acc[tq:] / l[tq:]).T)


def _attn(pb, pvt, qn_w, kn_w, lam_p, subln_w, *, layer, batch, seq, tq=512, tk=512):
    n = batch * seq
    tq = min(tq, seq)
    tk = min(tk, seq)
    nq = seq // (ATTN_GROUP * tq)
    assert tk % tq == 0 and seq % tk == 0 and seq % (ATTN_GROUP * tq) == 0 and seq <= POS_SPLIT * POS_SPLIT
    lam_init = 0.8 - 0.6 * math.exp(-0.3 * layer)
    blocks = 2 * _nbytes((ATTN_GROUP * tq, LANES), BF16) + 2 * _nbytes((seq, LANES), BF16)
    scratch = (3 * _nbytes((seq, LANES), BF16) + _nbytes((ONES_ROWS, seq), BF16) + 2 * _nbytes((seq, LANES), F32)
               + 6 * ATTN_GROUP * _nbytes((tq, LANES), BF16)
               + (1 + 2 * (tk // tq)) * _nbytes((tk, tq), F32)
               + 2 * ATTN_GROUP * _nbytes((LANES + ONES_ROWS, tq), F32)
               + 3 * _nbytes((2 * tq, LANES), F32) + 4 * _nbytes((tk, tq), F32))
    return pl.pallas_call(
        functools.partial(_attn_kernel, tk=tk, lam_init=lam_init),
        grid=(batch, HEADS, nq),
        in_specs=[
            pl.BlockSpec((None, ATTN_GROUP * tq, LANES), lambda b, h, i: (2 * HEADS + h, b * nq + i, 0)),
            pl.BlockSpec((None, seq, LANES), lambda b, h, i: (3 * HEADS + h, b, 0)),
            pl.BlockSpec((None, LANES, seq), lambda b, h, i: (HEADS + h, 0, b)),
            pl.BlockSpec((1, LANES), lambda b, h, i: (0, 0)),
            pl.BlockSpec((1, LANES), lambda b, h, i: (0, 0)),
            pl.BlockSpec((4, LANES // 2), lambda b, h, i: (0, 0)),
            pl.BlockSpec((LANES, 1), lambda b, h, i: (0, 0)),
        ],
        out_specs=pl.BlockSpec((None, ATTN_GROUP * tq, LANES), lambda b, h, i: (h, b * nq + i, 0)),
        out_shape=jax.ShapeDtypeStruct((HEADS, n, LANES), BF16),
        scratch_shapes=[
            pltpu.VMEM((seq, LANES), BF16),
            pltpu.VMEM((seq, LANES), BF16),
            pltpu.VMEM((LANES + ONES_ROWS, seq), BF16),
            pltpu.VMEM((seq, LANES), F32),
            pltpu.VMEM((seq, LANES), F32),
            pltpu.VMEM((tk // tq, tk, tq), F32),
            pltpu.VMEM((6 * ATTN_GROUP, tq, LANES), BF16),
            pltpu.VMEM((tk // tq, tk, tq), F32),
            pltpu.VMEM((2 * ATTN_GROUP, LANES + ONES_ROWS, tq), F32),
            pltpu.VMEM((2 * tq, 1), F32),
            pltpu.VMEM((2 * tq, 1), F32),
            pltpu.VMEM((2 * tq, LANES), F32),
        ],
        compiler_params=pltpu.CompilerParams(
            dimension_semantics=("arbitrary", "arbitrary", "arbitrary"),
            vmem_limit_bytes=_vmem_limit(blocks, scratch)),
        name="diffattn",
    )(pb, pb, pvt, jnp.tile(qn_w, 2).reshape(1, LANES), jnp.tile(kn_w, 2).reshape(1, LANES),
      lam_p, subln_w.reshape(LANES, 1))


def _outproj_kernel(ohg_ref, oda_ref, x_ref, w_ref, o_ref, mix_s):
    for c in range(HEADS):
        mix_s[:, c * LANES:(c + 1) * LANES] = ohg_ref[c]
        mix_s[:, (HEADS + c) * LANES:(HEADS + c + 1) * LANES] = oda_ref[c]
    o_ref[...] = x_ref[...] + _dot(mix_s[...], w_ref[...])


def _outproj(ohg, oda, x, w, layer, *, tm=512):
    n, d = x.shape
    width = w.shape[1]
    tm = min(tm, n)
    blocks = (2 * _nbytes((HEADS, tm, LANES), BF16) + 2 * _nbytes((tm, d), F32) + _nbytes((width, d), BF16))
    return pl.pallas_call(
        _outproj_kernel,
        grid=(n // tm,),
        in_specs=[
            pl.BlockSpec((HEADS, tm, LANES), lambda i: (0, i, 0)),
            pl.BlockSpec((HEADS, tm, LANES), lambda i: (0, i, 0)),
            pl.BlockSpec((tm, d), lambda i: (i, 0)),
            pl.BlockSpec((None, width, d), lambda i: (layer, 0, 0)),
        ],
        out_specs=pl.BlockSpec((tm, d), lambda i: (i, 0)),
        out_shape=jax.ShapeDtypeStruct((n, d), F32),
        scratch_shapes=[pltpu.VMEM((tm, width), BF16)],
        compiler_params=pltpu.CompilerParams(
            dimension_semantics=("parallel",),
            vmem_limit_bytes=_vmem_limit(blocks, _nbytes((tm, width), BF16))),
        name="outproj",
    )(ohg, oda, x, w)


def _ffn_kernel(x_ref, nw_ref, wg_ref, wu_ref, wd_ref, o_ref, h_s):
    @pl.when(pl.program_id(1) == 0)
    def _():
        x = x_ref[...]
        var = jnp.mean(x * x, axis=-1, keepdims=True)
        h_s[...] = (x * lax.rsqrt(var + EPS) * nw_ref[...]).astype(BF16)
        o_ref[...] = x

    h = h_s[...]
    g = _dot(h, wg_ref[...])
    u = _dot(h, wu_ref[...])
    o_ref[...] += _dot((g * _sigmoid(g) * u).astype(BF16), wd_ref[...])


def _ffn(x, nw, wg, wu, wd, layer, *, tm=1024, tf=512):
    n, d = x.shape
    ff = wg.shape[2]
    tm = min(tm, n)
    blocks = (2 * _nbytes((tm, d), F32) + 3 * _nbytes((d, tf), BF16))
    scratch = _nbytes((tm, d), BF16) + 3 * _nbytes((tm, tf), F32)
    return pl.pallas_call(
        _ffn_kernel,
        grid=(n // tm, ff // tf),
        in_specs=[
            pl.BlockSpec((tm, d), lambda i, f: (jnp.minimum(i + jnp.minimum(f, 1), n // tm - 1), 0)),
            pl.BlockSpec((1, d), lambda i, f: (0, 0)),
            pl.BlockSpec((None, d, tf), lambda i, f: (layer, 0, f)),
            pl.BlockSpec((None, d, tf), lambda i, f: (layer, 0, f)),
            pl.BlockSpec((None, tf, d), lambda i, f: (layer, f, 0)),
        ],
        out_specs=pl.BlockSpec((tm, d), lambda i, f: (i, 0)),
        out_shape=jax.ShapeDtypeStruct((n, d), F32),
        scratch_shapes=[pltpu.VMEM((tm, d), BF16)],
        compiler_params=pltpu.CompilerParams(
            dimension_semantics=("parallel", "arbitrary"),
            vmem_limit_bytes=_vmem_limit(blocks, scratch)),
        name="ffn",
    )(x, nw.reshape(1, d), wg, wu, wd)


def kernel(x, norm_mix_w, w_in, hg_lb_logits, hg_onorm_w, da_qnorm_w, da_knorm_w, da_lambda,
           da_subln_w, w_out, norm_ffn_w, w_gate, w_up, w_down):
    batch, seq, d = x.shape
    depth = w_in.shape[0]
    hg = d // 2
    assert hg == HEADS * LANES and seq % CHUNK == 0
    xf = x.reshape(batch * seq, d).astype(F32)
    lb_logits = hg_lb_logits.astype(F32).reshape(2 * depth, hg)

    tn = hg
    w_in_bf = w_in.astype(BF16)
    per_tile = tn // WT_TILE
    w_in_t = _transposed_columns(w_in, tuple(3 * per_tile + c for c in range(per_tile))
                                 + tuple(7 * per_tile + c for c in range(per_tile)))
    w_out_bf, w_gate_bf, w_up_bf, w_down_bf = (w.astype(BF16) for w in (w_out, w_gate, w_up, w_down))

    for l in range(depth):
        pb, pz, pvt = _proj(xf, norm_mix_w[l], w_in_bf, w_in_t, l, (0, 4, 5, 6), (1, 2), tn=tn)
        ohg = _hgrn(pb, pvt, pz, lb_logits, hg_onorm_w[l], layer=l, batch=batch, seq=seq)
        oda = _attn(pb, pvt, da_qnorm_w[l], da_knorm_w[l], da_lambda[l].astype(F32), da_subln_w[l],
                    layer=l, batch=batch, seq=seq)
        x1 = _outproj(ohg, oda, xf, w_out_bf, l)
        xf = _ffn(x1, norm_ffn_w[l], w_gate_bf, w_up_bf, w_down_bf, l)
    return xf.reshape(batch, seq, d).astype(x.dtype)
```
